```python
import math
import jax, jax.numpy as jnp
from jax import lax
import numpy as np

D_MODEL = 1024
BATCH = 32
SEQ = 2048
DEPTH = 2

GRID_W = 64
CTX_LEN = 256

SSD_HEADS = 16
SSD_HEAD_DIM = 64
SSD_INNER = SSD_HEADS * SSD_HEAD_DIM
SSD_GROUPS = 2
SSD_STATE = 128
SSD_CONV = 5
SSD_CHUNK = 128
SSD_XBC = SSD_INNER + 2 * SSD_GROUPS * SSD_STATE
SSD_COLS = SSD_XBC + 2 * SSD_HEADS
FFT_GROUPS = 4
FFT_GROUP_DIM = 128
FFT_WIDTH = FFT_GROUPS * FFT_GROUP_DIM
GMLP_GROUPS = 4
GMLP_GROUP_DIM = 128
GMLP_WIDTH = GMLP_GROUPS * GMLP_GROUP_DIM
GMLP_CHUNK = 128
N_BRANCH = 3
D_FF = 2816
FFN_CONV = 3
EPS = 1e-6

OFF_DT = SSD_XBC
OFF_Z = SSD_COLS
OFF_FFT = OFF_Z + SSD_INNER
OFF_GMLP = OFF_FFT + FFT_WIDTH
OFF_GATE = OFF_GMLP + 2 * GMLP_WIDTH
IN_COLS = OFF_GATE + N_BRANCH * D_MODEL
IN_SPLIT = (OFF_DT, OFF_Z, OFF_FFT, OFF_GMLP, OFF_GATE)

kernel_name = "hybrid_ssd_fnet_gmlp_convffn_dit"


def rmsnorm(x, w):
    xf = x.astype(jnp.float32)
    xf = xf * lax.rsqrt(jnp.mean(xf * xf, axis=-1, keepdims=True) + EPS)
    return (xf * w.astype(jnp.float32)).astype(x.dtype)


def modulate(h, shift, scale):
    return h * (1 + scale) + shift


def dwconv(x, w, b):
    k = w.shape[0]
    y = lax.conv_general_dilated(x, w[:, None, :].astype(x.dtype), window_strides=(1,),
                                 padding=[(k // 2, k // 2)],
                                 dimension_numbers=('NWC', 'WIO', 'NWC'),
                                 feature_group_count=x.shape[-1])
    return y + b


def ssd_scan(x, dt, a, bm, cm, init_state, with_y):
    b, L, H, P = x.shape
    G, N = bm.shape[2], bm.shape[3]
    R = H // G
    Q = SSD_CHUNK
    nc = L // Q
    xf = (x.astype(jnp.float32) * dt[..., None]).reshape(b, nc, Q, G, R, P)
    bf = bm.astype(jnp.float32).reshape(b, nc, Q, G, N)
    la = jnp.transpose((dt * a).reshape(b, nc, Q, G, R), (0, 3, 4, 1, 2))
    cum = jnp.cumsum(la, axis=-1)
    decay_to_end = jnp.exp(cum[..., -1:] - cum)
    states = jnp.einsum('bcqgn,bgrcq,bcqgrp->bcgrpn', bf, decay_to_end, xf)
    cpad = jnp.concatenate([jnp.zeros((b, G, R, 1), jnp.float32),
                            jnp.cumsum(cum[..., -1], axis=-1)], axis=-1)
    tri_c = jnp.tril(jnp.ones((nc + 1, nc + 1), dtype=bool))
    dchunk = jnp.exp(jnp.where(tri_c, cpad[..., :, None] - cpad[..., None, :], -jnp.inf))
    states_all = jnp.concatenate([init_state[:, None], states], axis=1)
    new_states = jnp.einsum('bgrzc,bcgrpn->bzgrpn', dchunk, states_all)
    final = new_states[:, -1]
    if not with_y:
        return None, final
    prev = new_states[:, :-1]
    cf = cm.astype(jnp.float32).reshape(b, nc, Q, G, N)
    tri_q = jnp.tril(jnp.ones((Q, Q), dtype=bool))
    lmat = jnp.exp(jnp.where(tri_q, cum[..., :, None] - cum[..., None, :], -jnp.inf))
    cb = jnp.einsum('bcqgn,bcsgn->bcgqs', cf, bf)
    y_diag = jnp.einsum('bcgqs,bgrcqs,bcsgrp->bcqgrp', cb, lmat, xf)
    y_off = jnp.einsum('bcqgn,bcgrpn,bgrcq->bcqgrp', cf, prev, jnp.exp(cum))
    y = (y_diag + y_off).reshape(b, L, H, P)
    return y.astype(x.dtype), final


def ssd_prep(xbc_raw, dt_raw, conv_w, conv_b, dt_bias):
    b, L, _ = xbc_raw.shape
    xbc = jax.nn.silu(dwconv(xbc_raw, conv_w, conv_b))
    xs, bm, cm = jnp.split(xbc, [SSD_INNER, SSD_INNER + SSD_GROUPS * SSD_STATE], axis=-1)
    xs = xs.reshape(b, L, SSD_HEADS, SSD_HEAD_DIM)
    bm = bm.reshape(b, L, SSD_GROUPS, SSD_STATE)
    cm = cm.reshape(b, L, SSD_GROUPS, SSD_STATE)
    dt = jax.nn.softplus(dt_raw.astype(jnp.float32).reshape(b, L, 2, SSD_HEADS)
                         + dt_bias.astype(jnp.float32))
    return xs, bm, cm, dt


def ssd_bidir(xs, bm, cm, dt, a_log, init_f, init_b, with_y):
    a = -jnp.exp(a_log.astype(jnp.float32))
    flip = lambda t: jnp.flip(t, axis=1)
    y_f, s_f = ssd_scan(xs, dt[:, :, 0], a[0], bm, cm, init_f, with_y)
    y_b, s_b = ssd_scan(flip(xs), flip(dt[:, :, 1]), a[1], flip(bm), flip(cm), init_b, with_y)
    if not with_y:
        return None, s_f, s_b
    return y_f + flip(y_b), s_f, s_b


def ssd_output(y, xs, z, d_skip, norm_w):
    b, L = z.shape[0], z.shape[1]
    y = (y + d_skip[:, None] * xs).reshape(b, L, SSD_INNER) * jax.nn.silu(z)
    y = rmsnorm(y.reshape(b, L, SSD_GROUPS, SSD_INNER // SSD_GROUPS),
                norm_w.reshape(SSD_GROUPS, SSD_INNER // SSD_GROUPS))
    return y.reshape(b, L, SSD_INNER)


def fourier_mix(f):
    b, L, _ = f.shape
    fg = f.astype(jnp.float32).reshape(b, L, FFT_GROUPS, FFT_GROUP_DIM)
    y = jnp.fft.fft2(fg, axes=(1, 3), norm='ortho').real
    return y.reshape(b, L, FFT_WIDTH).astype(f.dtype)


def spatial_gating(uv, w_s, b_s):
    b, L, _ = uv.shape
    u, v = jnp.split(jax.nn.gelu(uv), 2, axis=-1)
    nc = L // GMLP_CHUNK
    vg = v.reshape(b, nc, GMLP_CHUNK, GMLP_GROUPS, GMLP_GROUP_DIM)
    s = jnp.einsum('gqs,bcsgd->bcqgd', w_s, vg) + jnp.transpose(b_s)[:, :, None]
    return u * s.reshape(b, L, GMLP_WIDTH)


def merge_branches(y_ssd, y_fft, y_gmlp, gate_logits, w_ssd_o, w_fft_o, w_gmlp_o, w_out):
    g = jax.nn.sigmoid(gate_logits)
    g0, g1, g2 = jnp.split(g, N_BRANCH, axis=-1)
    m = g0 * (y_ssd @ w_ssd_o) + g1 * (y_fft @ w_fft_o) + g2 * (y_gmlp @ w_gmlp_o)
    return m @ w_out


def token_mixer(hl, hc, w_in, conv_w, conv_b, a_log, dt_bias, d_skip, ssd_norm_w,
                w_s, b_s, w_ssd_o, w_fft_o, w_gmlp_o, w_out, need_ctx):
    xbc_l, dt_l, z_l, f_l, uv_l, gate_l = jnp.split(hl @ w_in, IN_SPLIT, axis=-1)
    if need_ctx:
        xbc_c, dt_c, z_c, f_c, uv_c, gate_c = jnp.split(hc @ w_in, IN_SPLIT, axis=-1)
    else:
        xbc_c, dt_c = jnp.split(hc @ w_in[:, :SSD_COLS], [SSD_XBC], axis=-1)
    xs_c, bm_c, cm_c, dtc = ssd_prep(xbc_c, dt_c, conv_w, conv_b, dt_bias)
    zero = jnp.zeros((hc.shape[0], SSD_GROUPS, SSD_HEADS // SSD_GROUPS, SSD_HEAD_DIM, SSD_STATE),
                     jnp.float32)
    y_c, s_f, s_b = ssd_bidir(xs_c, bm_c, cm_c, dtc, a_log, zero, zero, need_ctx)
    xs_l, bm_l, cm_l, dtl = ssd_prep(xbc_l, dt_l, conv_w, conv_b, dt_bias)
    y_l, _, _ = ssd_bidir(xs_l, bm_l, cm_l, dtl, a_log, s_f, s_b, True)
    out_l = merge_branches(ssd_output(y_l, xs_l, z_l, d_skip, ssd_norm_w), fourier_mix(f_l),
                           spatial_gating(uv_l, w_s, b_s), gate_l, w_ssd_o, w_fft_o, w_gmlp_o, w_out)
    if not need_ctx:
        return out_l, None
    out_c = merge_branches(ssd_output(y_c, xs_c, z_c, d_skip, ssd_norm_w), fourier_mix(f_c),
                           spatial_gating(uv_c, w_s, b_s), gate_c, w_ssd_o, w_fft_o, w_gmlp_o, w_out)
    return out_l, out_c


def conv_ffn(h, w_up, conv_w, conv_b, w_down, on_grid):
    u = h @ w_up
    b, L, C = u.shape
    if on_grid:
        rows = L // GRID_W
        u = dwconv(u.reshape(b * rows, GRID_W, C), conv_w, conv_b).reshape(b, L, C)
    else:
        u = dwconv(u, conv_w, conv_b)
    a, v = jnp.split(u, 2, axis=-1)
    return (jax.nn.silu(a) * v) @ w_down


def setup_inputs(seed: int = 0) -> dict:
    key = jax.random.key(seed)
    ks = jax.random.split(key, 32)
    D, H = D_MODEL, SSD_HEADS
    nrm = lambda k, shape, fan_in, s=1.0: s * jax.random.normal(k, shape, jnp.float32) * fan_in ** -0.5
    dt0 = jnp.exp(jax.random.uniform(ks[10], (DEPTH, 2, H), jnp.float32, math.log(1e-3), math.log(1e-1)))
    return {
        "x": jax.random.normal(ks[0], (BATCH, SEQ, D), jnp.float32),
        "c": jax.random.normal(ks[1], (BATCH, D), jnp.float32),
        "ctx": jax.random.normal(ks[2], (BATCH, CTX_LEN, D), jnp.float32),
        "c_ctx": jax.random.normal(ks[3], (D,), jnp.float32),
        "w_mod": nrm(ks[4], (DEPTH, D, 6 * D), D, 0.5),
        "b_mod": 0.02 * jax.random.normal(ks[5], (DEPTH, 6 * D), jnp.float32),
        "norm1_w": 1.0 + 0.02 * jax.random.normal(ks[6], (DEPTH, D), jnp.float32),
        "w_in": nrm(ks[7], (DEPTH, D, IN_COLS), D),
        "ssd_conv_w": nrm(ks[8], (DEPTH, SSD_CONV, SSD_XBC), SSD_CONV),
        "ssd_conv_b": 0.02 * jax.random.normal(ks[9], (DEPTH, SSD_XBC), jnp.float32),
        "ssd_a_log": jnp.log(jax.random.uniform(ks[11], (DEPTH, 2, H), jnp.float32, 1.0, 16.0)),
        "ssd_dt_bias": dt0 + jnp.log(-jnp.expm1(-dt0)),
        "ssd_d": 1.0 + 0.02 * jax.random.normal(ks[12], (DEPTH, H), jnp.float32),
        "ssd_norm_w": 1.0 + 0.02 * jax.random.normal(ks[13], (DEPTH, SSD_INNER), jnp.float32),
        "gmlp_w_s": nrm(ks[14], (DEPTH, GMLP_GROUPS, GMLP_CHUNK, GMLP_CHUNK), GMLP_CHUNK),
        "gmlp_b_s": 1.0 + 0.02 * jax.random.normal(ks[15], (DEPTH, GMLP_GROUPS, GMLP_CHUNK), jnp.float32),
        "w_ssd_o": nrm(ks[16], (DEPTH, SSD_INNER, D), SSD_INNER),
        "w_fft_o": nrm(ks[17], (DEPTH, FFT_WIDTH, D), FFT_WIDTH),
        "w_gmlp_o": nrm(ks[18], (DEPTH, GMLP_WIDTH, D), GMLP_WIDTH),
        "w_out": nrm(ks[19], (DEPTH, D, D), D),
        "norm2_w": 1.0 + 0.02 * jax.random.normal(ks[20], (DEPTH, D), jnp.float32),
        "ffn_w_up": nrm(ks[21], (DEPTH, D, 2 * D_FF), D),
        "ffn_conv_w": nrm(ks[22], (DEPTH, FFN_CONV, 2 * D_FF), FFN_CONV),
        "ffn_conv_b": 0.02 * jax.random.normal(ks[23], (DEPTH, 2 * D_FF), jnp.float32),
        "ffn_w_down": nrm(ks[24], (DEPTH, D_FF, D), D_FF),
        "final_norm_w": 1.0 + 0.02 * jax.random.normal(ks[25], (D,), jnp.float32),
    }


def reference(x, c, ctx, c_ctx, w_mod, b_mod, norm1_w, w_in, ssd_conv_w, ssd_conv_b, ssd_a_log,
              ssd_dt_bias, ssd_d, ssd_norm_w, gmlp_w_s, gmlp_b_s, w_ssd_o, w_fft_o, w_gmlp_o, w_out,
              norm2_w, ffn_w_up, ffn_conv_w, ffn_conv_b, ffn_w_down, final_norm_w):
    xl, xc = x, ctx
    for i in range(DEPTH):
        need_ctx = i < DEPTH - 1
        mod_l = (jax.nn.silu(c) @ w_mod[i] + b_mod[i])[:, None, :]
        mod_c = jax.nn.silu(c_ctx) @ w_mod[i] + b_mod[i]
        sh1, sc1, g1, sh2, sc2, g2 = jnp.split(mod_l, 6, axis=-1)
        csh1, csc1, cg1, csh2, csc2, cg2 = jnp.split(mod_c, 6, axis=-1)
        hl = modulate(rmsnorm(xl, norm1_w[i]), sh1, sc1)
        hc = modulate(rmsnorm(xc, norm1_w[i]), csh1, csc1)
        out_l, out_c = token_mixer(hl, hc, w_in[i], ssd_conv_w[i], ssd_conv_b[i], ssd_a_log[i],
                                   ssd_dt_bias[i], ssd_d[i], ssd_norm_w[i], gmlp_w_s[i], gmlp_b_s[i],
                                   w_ssd_o[i], w_fft_o[i], w_gmlp_o[i], w_out[i], need_ctx)
        xl = xl + g1 * out_l
        hl2 = modulate(rmsnorm(xl, norm2_w[i]), sh2, sc2)
        xl = xl + g2 * conv_ffn(hl2, ffn_w_up[i], ffn_conv_w[i], ffn_conv_b[i], ffn_w_down[i], True)
        if need_ctx:
            xc = xc + cg1 * out_c
            hc2 = modulate(rmsnorm(xc, norm2_w[i]), csh2, csc2)
            xc = xc + cg2 * conv_ffn(hc2, ffn_w_up[i], ffn_conv_w[i], ffn_conv_b[i], ffn_w_down[i], False)
    return rmsnorm(xl, final_norm_w)
```

```python
import functools
import math

import numpy as np
import jax
import jax.numpy as jnp
from jax import lax
from jax.experimental import pallas as pl
from jax.experimental.pallas import tpu as pltpu

F32 = jnp.float32
BF16 = jnp.bfloat16

D_MODEL = 1024
GRID_W = 64
SSD_HEADS = 16
SSD_HEAD_DIM = 64
SSD_INNER = SSD_HEADS * SSD_HEAD_DIM
SSD_GROUPS = 2
SSD_STATE = 128
SSD_CONV = 5
SSD_CHUNK = 128
SSD_XBC = SSD_INNER + 2 * SSD_GROUPS * SSD_STATE
SSD_COLS = SSD_XBC + 2 * SSD_HEADS
FFT_GROUPS = 4
FFT_GROUP_DIM = 128
FFT_WIDTH = FFT_GROUPS * FFT_GROUP_DIM
GMLP_GROUPS = 4
GMLP_GROUP_DIM = 128
GMLP_WIDTH = GMLP_GROUPS * GMLP_GROUP_DIM
GMLP_CHUNK = 128
N_BRANCH = 3
D_FF = 2816
FFN_CONV = 3
EPS = 1e-6

OFF_DT = SSD_XBC
OFF_Z = SSD_COLS
OFF_FFT = OFF_Z + SSD_INNER
OFF_GMLP = OFF_FFT + FFT_WIDTH
OFF_GATE = OFF_GMLP + 2 * GMLP_WIDTH

V7X_LANES = 128
V7X_SUBLANES = 8
V7X_VMEM_LIMIT_BYTES = 56 * 1024 * 1024

HEAD_REP = 3
HALO = V7X_SUBLANES


def _dot(a, b):
    return jnp.dot(a, b, preferred_element_type=F32)


def _sigmoid(x):
    return 1.0 / (1.0 + jnp.exp(-x))


def _silu(x):
    return x * _sigmoid(x)


def _gelu_tanh(x):
    return 0.5 * x * (1.0 + jnp.tanh(math.sqrt(2.0 / math.pi) * (x + 0.044715 * (x * x * x))))


def _softplus(x):
    return jnp.maximum(x, 0.0) + jnp.log(1.0 + jnp.exp(-jnp.abs(x)))


def _params(n_grid):
    return pltpu.CompilerParams(dimension_semantics=("arbitrary",) * n_grid,
                                vmem_limit_bytes=V7X_VMEM_LIMIT_BYTES)


def _resident(shape):
    nd = len(shape)
    return pl.BlockSpec(shape, lambda *_: (0,) * nd, pipeline_mode=pl.Buffered(1))


def _mod_kernel(c_ref, w_ref, b_ref, o_ref):
    s = _silu(c_ref[...]).astype(BF16)
    o_ref[0] = _dot(s, w_ref[0].astype(BF16)) + b_ref[0]


def _modulation(cc, w_mod, b_mod):
    depth, d, n = w_mod.shape
    rows = cc.shape[0]
    tn = 1536
    return pl.pallas_call(
        _mod_kernel,
        out_shape=jax.ShapeDtypeStruct((depth, rows, n), F32),
        grid=(depth, n // tn),
        in_specs=[pl.BlockSpec((rows, d), lambda l, j: (0, 0)),
                  pl.BlockSpec((1, d, tn), lambda l, j: (l, 0, j)),
                  pl.BlockSpec((1, 1, tn), lambda l, j: (l, 0, j))],
        out_specs=pl.BlockSpec((1, rows, tn), lambda l, j: (l, 0, j)),
        compiler_params=_params(2),
        name="modulation",
    )(cc, w_mod, b_mod.reshape(depth, 1, n))


def _norm_modulate(x_ref, h_ref, nw, sh, sc1, n_rows, rc):
    def body(r, carry):
        rows = pl.ds(pl.multiple_of(r * rc, rc), rc)
        xf = x_ref[rows, :]
        ms = jnp.mean(xf * xf, axis=-1, keepdims=True)
        hn = (xf * lax.rsqrt(ms + EPS)) * nw
        h_ref[rows, :] = (hn * sc1 + sh).astype(BF16)
        return carry
    lax.fori_loop(0, n_rows // rc, body, 0)


def _segment_masks(rc, width, taps, seg, n_rows):
    if seg >= n_rows:
        return None
    assert rc % seg == 0 and seg & (seg - 1) == 0
    pos = lax.broadcasted_iota(jnp.int32, (rc, width), 0) & (seg - 1)
    half = taps // 2
    return {off: (pos + off >= 0) & (pos + off < seg) for off in range(-half, half + 1) if off != 0}


def _row_conv(acc_ref, r0, rc, taps, cw, cb, masks):
    half = taps // 2
    out = None
    for k in range(taps):
        off = k - half
        v = acc_ref[pl.ds(HALO + r0 + off, rc), :]
        if masks is not None and off != 0:
            v = jnp.where(masks[off], v, 0.0)
        term = v * cw[k:k + 1, :]
        out = term if out is None else out + term
    return out + cb


def _zero_halo(acc_ref, n_rows):
    z = jnp.zeros((HALO, acc_ref.shape[1]), F32)
    acc_ref[0:HALO, :] = z
    acc_ref[HALO + n_rows:2 * HALO + n_rows, :] = z


def _inproj_a_kernel(x_ref, sh_ref, sc_ref, nw_ref, w_ref, cw_ref, cb_ref, wdt_ref, dtb_ref,
                     h_ref, xbc_ref, dt_ref, acc_ref, *, n_rows, rc, tn):
    _norm_modulate(x_ref, h_ref, nw_ref[...], sh_ref[0], 1.0 + sc_ref[0], n_rows, rc)
    _zero_halo(acc_ref, n_rows)
    mb = min(512, n_rows)

    def dt_body(r, carry):
        rows = pl.ds(pl.multiple_of(r * mb, mb), mb)
        dt_ref[rows, :] = _softplus(_dot(h_ref[rows, :], wdt_ref[...]) + dtb_ref[...])
        return carry
    lax.fori_loop(0, n_rows // mb, dt_body, 0)

    for n in range(SSD_XBC // tn):
        cols = slice(n * tn, (n + 1) * tn)

        def mm_body(r, carry):
            rows = pl.multiple_of(r * mb, mb)
            acc_ref[pl.ds(HALO + rows, mb), :] = _dot(h_ref[pl.ds(rows, mb), :], w_ref[:, cols])
            return carry
        lax.fori_loop(0, n_rows // mb, mm_body, 0)
        cw = cw_ref[:, cols]
        cb = cb_ref[:, cols]
        for r in range(n_rows // rc):
            u = _row_conv(acc_ref, r * rc, rc, SSD_CONV, cw, cb, None)
            xbc_ref[r * rc:(r + 1) * rc, cols] = _silu(u).astype(BF16)


def _inproj_a(x2, mod3, mod_row, nw, w_xbc, cw, cb, w_dt, dt_b, seq):
    nt, d = x2.shape
    nb = nt // seq
    tn = 512
    rc = min(256, seq)
    ndt = w_dt.shape[1]
    kern = functools.partial(_inproj_a_kernel, n_rows=seq, rc=rc, tn=tn)
    return pl.pallas_call(
        kern,
        out_shape=(jax.ShapeDtypeStruct((nt, d), BF16),
                   jax.ShapeDtypeStruct((nt, SSD_XBC), BF16),
                   jax.ShapeDtypeStruct((nt, ndt), F32)),
        grid=(nb,),
        in_specs=[pl.BlockSpec((seq, d), lambda i: (i, 0)),
                  pl.BlockSpec((1, 1, d), lambda i: (mod_row(i), 0, 0)),
                  pl.BlockSpec((1, 1, d), lambda i: (mod_row(i), 0, 1)),
                  _resident((1, d)),
                  _resident((d, SSD_XBC)),
                  _resident(cw.shape),
                  _resident((1, SSD_XBC)),
                  _resident((d, ndt)),
                  _resident((1, ndt))],
        out_specs=(pl.BlockSpec((seq, d), lambda i: (i, 0)),
                   pl.BlockSpec((seq, SSD_XBC), lambda i: (i, 0)),
                   pl.BlockSpec((seq, ndt), lambda i: (i, 0))),
        scratch_shapes=[pltpu.VMEM((seq + 2 * HALO, tn), F32)],
        compiler_params=_params(1),
        name="inproj_a",
    )(x2, mod3, mod3, nw, w_xbc, cw, cb, w_dt, dt_b)


_REST_Z = 0
_REST_F = SSD_INNER
_REST_UV = _REST_F + FFT_WIDTH
_REST_GATE = _REST_UV + 2 * GMLP_WIDTH
_REST_COLS = _REST_GATE + N_BRANCH * D_MODEL


def _inproj_b_kernel(h_ref, w_ref, z_ref, f_ref, uv_ref, g_ref, *, tn):
    h = h_ref[...]
    for n in range(_REST_COLS // tn):
        c0 = n * tn
        r = _dot(h, w_ref[:, c0:c0 + tn])
        if c0 < _REST_F:
            z_ref[:, c0:c0 + tn] = _silu(r).astype(BF16)
        elif c0 < _REST_UV:
            f_ref[:, c0 - _REST_F:c0 - _REST_F + tn] = r.astype(BF16)
        elif c0 < _REST_GATE:
            uv_ref[:, c0 - _REST_UV:c0 - _REST_UV + tn] = _gelu_tanh(r).astype(BF16)
        else:
            g_ref[:, c0 - _REST_GATE:c0 - _REST_GATE + tn] = _sigmoid(r).astype(BF16)


def _inproj_b(h2, w_rest, tm):
    nt, d = h2.shape
    tn = 512
    widths = (SSD_INNER, FFT_WIDTH, 2 * GMLP_WIDTH, N_BRANCH * D_MODEL)
    return pl.pallas_call(
        functools.partial(_inproj_b_kernel, tn=tn),
        out_shape=tuple(jax.ShapeDtypeStruct((nt, w), BF16) for w in widths),
        grid=(nt // tm,),
        in_specs=[pl.BlockSpec((tm, d), lambda i: (i, 0)), _resident((d, _REST_COLS))],
        out_specs=tuple(pl.BlockSpec((tm, w), lambda i: (i, 0)) for w in widths),
        compiler_params=_params(1),
        name="inproj_b",
    )(h2, w_rest)


def _split3_pack(v, lane):
    hi = v.astype(BF16).astype(F32)
    r1 = v - hi
    mid = r1.astype(BF16).astype(F32)
    lo = r1 - mid
    return jnp.where(lane < SSD_HEADS, hi, jnp.where(lane < 2 * SSD_HEADS, mid, lo)).astype(BF16)


def _split3_stack(v):
    hi = v.astype(BF16)
    r1 = v - hi.astype(F32)
    mid = r1.astype(BF16)
    lo = (r1 - mid.astype(F32)).astype(BF16)
    return jnp.concatenate([hi, mid, lo], axis=0)


def _ssd_kernel(xbc_ref, dt_ref, z_ref, alog_ref, dskip_ref, nw_ref, e_ref, if_ref, ib_ref,
                *refs, n_rows, with_y):
    if with_y:
        y_ref, ff_ref, fb_ref, yacc_ref, st_ref = refs
    else:
        ff_ref, fb_ref, st_ref = refs
        y_ref = yacc_ref = None
    q = SSD_CHUNK
    nc = n_rows // q
    gw = SSD_INNER // SSD_GROUPS
    hpg = SSD_HEADS // SSD_GROUPS

    ri = lax.broadcasted_iota(jnp.int32, (q, q), 0)
    ci = lax.broadcasted_iota(jnp.int32, (q, q), 1)
    lane = ci
    low_mask = ci <= ri
    up_mask = ci >= ri
    tri_low = jnp.where(low_mask, 1.0, 0.0).astype(BF16)
    tri_up = jnp.where(up_mask, 1.0, 0.0).astype(BF16)
    tri3 = (jnp.concatenate([tri_low] * 3, axis=1), jnp.concatenate([tri_up] * 3, axis=1))
    masks = (low_mask, up_mask)
    first_half = lane < SSD_HEAD_DIM
    lane8 = lax.broadcasted_iota(jnp.int32, (V7X_SUBLANES, q), 1)

    if with_y:
        yacc_ref[...] = jnp.zeros(yacc_ref.shape, F32)

    for d in range(2):
        st_ref[...] = (if_ref if d == 0 else ib_ref)[0]
        a_row = -jnp.exp(alog_ref[d:d + 1, :])
        mask = masks[d]

        def chunk(k, carry, d=d, a_row=a_row, mask=mask):
            c = k if d == 0 else nc - 1 - k
            rows = pl.ds(pl.multiple_of(c * q, q), q)
            dt_c = dt_ref[rows, d * q:(d + 1) * q]
            la = dt_c * a_row
            cum = _dot(tri3[d], _split3_stack(la))
            tot = cum[q - 1:q, :] if d == 0 else cum[0:1, :]
            w = dt_c * jnp.exp(tot - cum)
            w_x = _dot(_split3_pack(w, lane), e_ref[...])
            etot = jnp.broadcast_to(jnp.exp(tot), (V7X_SUBLANES, q))
            etot_x = _dot(_split3_pack(etot, lane8), e_ref[...])[0:1, :]
            xs = xbc_ref[rows, 0:SSD_INNER]
            wx = (xs.astype(F32) * w_x).astype(BF16)
            if with_y:
                cum_t = cum.T
                dt_t = dt_c.T
                ecum = jnp.exp(cum)
            for g in range(SSD_GROUPS):
                bm = xbc_ref[rows, SSD_INNER + g * SSD_STATE:SSD_INNER + (g + 1) * SSD_STATE]
                cm = xbc_ref[rows, SSD_INNER + (SSD_GROUPS + g) * SSD_STATE:
                             SSD_INNER + (SSD_GROUPS + g + 1) * SSD_STATE]
                st = st_ref[g]
                if with_y:
                    cb = lax.dot_general(cm, bm, (((1,), (1,)), ((), ())), preferred_element_type=F32)
                    cm_f = cm.astype(F32)
                    st_b = st.astype(BF16)
                    for j in range(hpg // 2):
                        pair = g * (hpg // 2) + j
                        rhs = jnp.concatenate(
                            [xs[:, pair * q:(pair + 1) * q], st_b[:, j * q:(j + 1) * q]], axis=0)
                        ys = []
                        for t in range(2):
                            h = 2 * pair + t
                            col = cum[:, h:h + 1]
                            diff = jnp.minimum(col - cum_t[h:h + 1, :], 0.0)
                            lm = jnp.where(mask, jnp.exp(diff), 0.0)
                            m_h = (cb * lm) * dt_t[h:h + 1, :]
                            cm_s = cm_f * ecum[:, h:h + 1]
                            lhs = jnp.concatenate([m_h.astype(BF16), cm_s.astype(BF16)], axis=1)
                            ys.append(_dot(lhs, rhs))
                        y_pair = jnp.where(first_half, ys[0], ys[1])
                        yacc_ref[rows, pair * q:(pair + 1) * q] += y_pair
                s_new = lax.dot_general(bm, wx[:, g * gw:(g + 1) * gw], (((0,), (0,)), ((), ())),
                                        preferred_element_type=F32)
                st_ref[g] = st * etot_x[:, g * gw:(g + 1) * gw] + s_new
            return carry

        lax.fori_loop(0, nc, chunk, 0)
        (ff_ref if d == 0 else fb_ref)[0] = st_ref[...]

    if with_y:
        dsk = dskip_ref[...]
        nw = nw_ref[...]

        def fin(c, carry):
            rows = pl.ds(pl.multiple_of(c * q, q), q)
            xs = xbc_ref[rows, 0:SSD_INNER].astype(F32)
            y = (yacc_ref[rows, :] + dsk * xs) * z_ref[rows, :].astype(F32)
            for g in range(SSD_GROUPS):
                yg = y[:, g * gw:(g + 1) * gw]
                ms = jnp.mean(yg * yg, axis=-1, keepdims=True)
                y_ref[rows, g * gw:(g + 1) * gw] = (
                    (yg * lax.rsqrt(ms + EPS)) * nw[:, g * gw:(g + 1) * gw]).astype(BF16)
            return carry
        lax.fori_loop(0, nc, fin, 0)


def _ssd(xbc, dt, zs, alog, dskip, nw, e3, init_f, init_b, seq, with_y):
    nt = xbc.shape[0]
    nb = nt // seq
    ndt = dt.shape[1]
    st_shape = (SSD_GROUPS, SSD_STATE, SSD_INNER // SSD_GROUPS)
    st_spec = pl.BlockSpec((1,) + st_shape, lambda i: (i, 0, 0, 0))
    st_sds = jax.ShapeDtypeStruct((nb,) + st_shape, F32)
    in_specs = [pl.BlockSpec((seq, SSD_XBC), lambda i: (i, 0)),
                pl.BlockSpec((seq, ndt), lambda i: (i, 0)),
                pl.BlockSpec((seq, SSD_INNER), lambda i: (i, 0)),
                _resident(alog.shape), _resident((1, SSD_INNER)), _resident((1, SSD_INNER)),
                _resident(e3.shape), st_spec, st_spec]
    out_shape = [st_sds, st_sds]
    out_specs = [st_spec, st_spec]
    scratch = [pltpu.VMEM(st_shape, F32)]
    if with_y:
        out_shape = [jax.ShapeDtypeStruct((nt, SSD_INNER), BF16)] + out_shape
        out_specs = [pl.BlockSpec((seq, SSD_INNER), lambda i: (i, 0))] + out_specs
        scratch = [pltpu.VMEM((seq, SSD_INNER), F32)] + scratch
    return pl.pallas_call(
        functools.partial(_ssd_kernel, n_rows=seq, with_y=with_y),
        out_shape=tuple(out_shape),
        grid=(nb,),
        in_specs=in_specs,
        out_specs=tuple(out_specs),
        scratch_shapes=scratch,
        compiler_params=_params(1),
        name="ssd_y" if with_y else "ssd_state",
    )(xbc, dt, zs, alog, dskip, nw, e3, init_f, init_b)


def _fft_kernel(f_ref, cs_ref, dl_ref, o_ref, r_ref, *, n_rows):
    gd = FFT_GROUP_DIM
    mb = min(256, n_rows)

    def stage1(r, carry):
        rows = pl.ds(pl.multiple_of(r * mb, mb), mb)
        for g in range(FFT_GROUPS):
            t = _dot(f_ref[rows, g * gd:(g + 1) * gd], cs_ref[...])
            r_ref[rows, g * gd:(g + 1) * gd] = t[:, :gd].astype(BF16)
            r_ref[pl.ds(pl.multiple_of(n_rows + r * mb, mb), mb), g * gd:(g + 1) * gd] = t[:, gd:].astype(BF16)
        return carry
    lax.fori_loop(0, n_rows // mb, stage1, 0)

    def stage2(r, carry):
        rows = pl.ds(pl.multiple_of(r * mb, mb), mb)
        o_ref[rows, :] = _dot(dl_ref[rows, :], r_ref[...]).astype(BF16)
        return carry
    lax.fori_loop(0, n_rows // mb, stage2, 0)


def _dft_constants(seq):
    k = np.arange(seq, dtype=np.float64)
    ang = 2.0 * np.pi * np.outer(k, k) / seq
    dl = np.concatenate([np.cos(ang), -np.sin(ang)], axis=1) / math.sqrt(seq)
    c = np.arange(FFT_GROUP_DIM, dtype=np.float64)
    angc = 2.0 * np.pi * np.outer(c, c) / FFT_GROUP_DIM
    cs = np.concatenate([np.cos(angc), np.sin(angc)], axis=1) / math.sqrt(FFT_GROUP_DIM)
    return jnp.asarray(cs, BF16), jnp.asarray(dl, BF16)


def _fft(f2, seq):
    nt = f2.shape[0]
    cs, dl = _dft_constants(seq)
    return pl.pallas_call(
        functools.partial(_fft_kernel, n_rows=seq),
        out_shape=jax.ShapeDtypeStruct((nt, FFT_WIDTH), BF16),
        grid=(nt // seq,),
        in_specs=[pl.BlockSpec((seq, FFT_WIDTH), lambda i: (i, 0)),
                  _resident(cs.shape), _resident(dl.shape)],
        out_specs=pl.BlockSpec((seq, FFT_WIDTH), lambda i: (i, 0)),
        scratch_shapes=[pltpu.VMEM((2 * seq, FFT_WIDTH), BF16)],
        compiler_params=_params(1),
        name="fourier_mix",
    )(f2, cs, dl)


def _merge_kernel(ys_ref, yf_ref, uv_ref, g_ref, x_ref, gm_ref, w1_ref, w2_ref, w3_ref, wo_ref,
                  ws_ref, bs_ref, o_ref, yg_ref, m_ref, *, tm):
    gd = GMLP_GROUP_DIM
    for c in range(tm // GMLP_CHUNK):
        rows = slice(c * GMLP_CHUNK, (c + 1) * GMLP_CHUNK)
        for g in range(GMLP_GROUPS):
            v = uv_ref[rows, GMLP_WIDTH + g * gd:GMLP_WIDTH + (g + 1) * gd]
            s = _dot(ws_ref[g], v) + bs_ref[g]
            u = uv_ref[rows, g * gd:(g + 1) * gd].astype(F32)
            yg_ref[rows, g * gd:(g + 1) * gd] = (u * s).astype(BF16)
    tn = 512
    for n in range(D_MODEL // tn):
        cols = slice(n * tn, (n + 1) * tn)
        m = g_ref[:, n * tn:(n + 1) * tn].astype(F32) * _dot(ys_ref[...], w1_ref[:, cols])
        m += g_ref[:, D_MODEL + n * tn:D_MODEL + (n + 1) * tn].astype(F32) * _dot(yf_ref[...], w2_ref[:, cols])
        m += g_ref[:, 2 * D_MODEL + n * tn:2 * D_MODEL + (n + 1) * tn].astype(F32) * _dot(yg_ref[...], w3_ref[:, cols])
        m_ref[:, cols] = m.astype(BF16)
    o_ref[...] = x_ref[...] + gm_ref[0] * _dot(m_ref[...], wo_ref[...])


def _merge(ys, yf, uv, gate, x2, mod3, mod_row, w1, w2, w3, wo, ws, bs, seq, tm):
    nt, d = x2.shape
    per_seq = seq // tm
    row = lambda i: (i, 0)
    return pl.pallas_call(
        functools.partial(_merge_kernel, tm=tm),
        out_shape=jax.ShapeDtypeStruct((nt, d), F32),
        grid=(nt // tm,),
        in_specs=[pl.BlockSpec((tm, SSD_INNER), row),
                  pl.BlockSpec((tm, FFT_WIDTH), row),
                  pl.BlockSpec((tm, 2 * GMLP_WIDTH), row),
                  pl.BlockSpec((tm, N_BRANCH * d), row),
                  pl.BlockSpec((tm, d), row),
                  pl.BlockSpec((1, 1, d), lambda i: (mod_row(i // per_seq), 0, 2)),
                  _resident(w1.shape), _resident(w2.shape), _resident(w3.shape), _resident(wo.shape),
                  _resident(ws.shape), _resident(bs.shape)],
        out_specs=pl.BlockSpec((tm, d), row),
        scratch_shapes=[pltpu.VMEM((tm, GMLP_WIDTH), BF16), pltpu.VMEM((tm, d), BF16)],
        compiler_params=_params(1),
        name="merge",
    )(ys, yf, uv, gate, x2, mod3, w1, w2, w3, wo, ws, bs)


def _ffn_kernel(x_ref, sh_ref, sc_ref, gm_ref, nw_ref, wu_ref, cw_ref, cb_ref, wd_ref, fw_ref,
                o_ref, h_ref, acca_ref, accv_ref, act_ref, *, tm, tf, seg, final_norm):
    rc = min(256, tm)
    _norm_modulate(x_ref, h_ref, nw_ref[...], sh_ref[0], 1.0 + sc_ref[0], tm, rc)
    _zero_halo(acca_ref, tm)
    _zero_halo(accv_ref, tm)
    masks = _segment_masks(rc, tf, FFN_CONV, seg, tm)
    h = h_ref[...]
    for c in range(D_FF // tf):
        ca = slice(c * tf, (c + 1) * tf)
        cv = slice(D_FF + c * tf, D_FF + (c + 1) * tf)
        acca_ref[HALO:HALO + tm, :] = _dot(h, wu_ref[:, ca])
        accv_ref[HALO:HALO + tm, :] = _dot(h, wu_ref[:, cv])
        for r in range(tm // rc):
            a = _row_conv(acca_ref, r * rc, rc, FFN_CONV, cw_ref[:, ca], cb_ref[:, ca], masks)
            v = _row_conv(accv_ref, r * rc, rc, FFN_CONV, cw_ref[:, cv], cb_ref[:, cv], masks)
            act_ref[r * rc:(r + 1) * rc, ca] = (_silu(a) * v).astype(BF16)
    y = x_ref[...] + gm_ref[0] * _dot(act_ref[...], wd_ref[...])
    if final_norm:
        ms = jnp.mean(y * y, axis=-1, keepdims=True)
        y = (y * lax.rsqrt(ms + EPS)) * fw_ref[...]
    o_ref[...] = y


def _ffn(x2, mod3, mod_row, nw, wu, cw, cb, wd, fw, seq, seg, tm, final_norm):
    nt, d = x2.shape
    per_seq = seq // tm
    tf = 256
    mrow = lambda k: (lambda i: (mod_row(i // per_seq), 0, k))
    return pl.pallas_call(
        functools.partial(_ffn_kernel, tm=tm, tf=tf, seg=seg, final_norm=final_norm),
        out_shape=jax.ShapeDtypeStruct((nt, d), F32),
        grid=(nt // tm,),
        in_specs=[pl.BlockSpec((tm, d), lambda i: (i, 0)),
                  pl.BlockSpec((1, 1, d), mrow(3)),
                  pl.BlockSpec((1, 1, d), mrow(4)),
                  pl.BlockSpec((1, 1, d), mrow(5)),
                  _resident((1, d)), _resident(wu.shape), _resident(cw.shape), _resident(cb.shape),
                  _resident(wd.shape), _resident((1, d))],
        out_specs=pl.BlockSpec((tm, d), lambda i: (i, 0)),
        scratch_shapes=[pltpu.VMEM((tm, d), BF16),
                        pltpu.VMEM((tm + 2 * HALO, tf), F32),
                        pltpu.VMEM((tm + 2 * HALO, tf), F32),
                        pltpu.VMEM((tm, D_FF), BF16)],
        compiler_params=_params(1),
        name="conv_ffn",
    )(x2, mod3, mod3, mod3, nw, wu, cw, cb, wd, fw)


def _rep_heads(v):
    pad = V7X_LANES - HEAD_REP * SSD_HEADS
    rep = jnp.concatenate([v] * HEAD_REP, axis=-1)
    return jnp.pad(rep, [(0, 0)] * (v.ndim - 1) + [(0, pad)])


def _head_expand_matrix():
    e = np.zeros((V7X_LANES, SSD_INNER), np.float32)
    for k in range(HEAD_REP * SSD_HEADS):
        h = k % SSD_HEADS
        e[k, h * SSD_HEAD_DIM:(h + 1) * SSD_HEAD_DIM] = 1.0
    return jnp.asarray(e, BF16)


def _pad_rows(a, rows):
    return jnp.pad(a, ((0, rows - a.shape[0]), (0, 0)))


def kernel(x, c, ctx, c_ctx, w_mod, b_mod, norm1_w, w_in, ssd_conv_w, ssd_conv_b, ssd_a_log,
           ssd_dt_bias, ssd_d, ssd_norm_w, gmlp_w_s, gmlp_b_s, w_ssd_o, w_fft_o, w_gmlp_o, w_out,
           norm2_w, ffn_w_up, ffn_conv_w, ffn_conv_b, ffn_w_down, final_norm_w):
    nb, seq, d = x.shape
    cseq = ctx.shape[1]
    depth = w_mod.shape[0]
    mod_rows = -(-(nb + 1) // V7X_SUBLANES) * V7X_SUBLANES
    cc = jnp.concatenate([c, c_ctx[None, :], jnp.zeros((mod_rows - nb - 1, d), F32)], axis=0)
    mod = _modulation(cc, w_mod, b_mod)
    lat_row = lambda i: i
    ctx_row = lambda i: nb
    e3 = _head_expand_matrix()
    tm_lat = 512
    tm_ctx = cseq

    xl = x.reshape(nb * seq, d)
    xc = ctx.reshape(nb * cseq, d)
    for i in range(depth):
        need_ctx = i < depth - 1
        mod3 = mod[i].reshape(mod_rows, 1, 6 * d)
        wi = w_in[i]
        w_xbc = wi[:, :SSD_XBC].astype(BF16)
        w_dt = jnp.concatenate([_rep_heads(wi[:, OFF_DT:OFF_DT + SSD_HEADS]),
                                _rep_heads(wi[:, OFF_DT + SSD_HEADS:OFF_Z])], axis=1).astype(BF16)
        dt_b = jnp.concatenate([_rep_heads(ssd_dt_bias[i, 0][None]), _rep_heads(ssd_dt_bias[i, 1][None])], axis=1)
        w_rest = wi[:, OFF_Z:].astype(BF16)
        cw = _pad_rows(ssd_conv_w[i], V7X_SUBLANES)
        cb = ssd_conv_b[i][None, :]
        alog = _pad_rows(_rep_heads(ssd_a_log[i]), V7X_SUBLANES)
        dskip = jnp.repeat(ssd_d[i], SSD_HEAD_DIM)[None, :]
        snw = ssd_norm_w[i][None, :]
        nw1 = norm1_w[i][None, :]
        nw2 = norm2_w[i][None, :]
        w1 = w_ssd_o[i].astype(BF16)
        w2 = w_fft_o[i].astype(BF16)
        w3 = w_gmlp_o[i].astype(BF16)
        wo = w_out[i].astype(BF16)
        ws = gmlp_w_s[i].astype(BF16)
        bs = jnp.broadcast_to(gmlp_b_s[i][:, :, None], (GMLP_GROUPS, GMLP_CHUNK, GMLP_GROUP_DIM))
        wu = ffn_w_up[i].astype(BF16)
        fcw = _pad_rows(ffn_conv_w[i], V7X_SUBLANES)
        fcb = ffn_conv_b[i][None, :]
        wd = ffn_w_down[i].astype(BF16)
        fw = final_norm_w[None, :]

        hc, xbc_c, dt_c = _inproj_a(xc, mod3, ctx_row, nw1, w_xbc, cw, cb, w_dt, dt_b, cseq)
        zero_state = jnp.zeros((nb, SSD_GROUPS, SSD_STATE, SSD_INNER // SSD_GROUPS), F32)
        if need_ctx:
            z_c, f_c, uv_c, g_c = _inproj_b(hc, w_rest, tm_ctx)
            y_c, s_f, s_b = _ssd(xbc_c, dt_c, z_c, alog, dskip, snw, e3, zero_state, zero_state, cseq, True)
        else:
            s_f, s_b = _ssd(xbc_c, dt_c, xbc_c[:, :SSD_INNER], alog, dskip, snw, e3, zero_state, zero_state,
                            cseq, False)

        hl, xbc_l, dt_l = _inproj_a(xl, mod3, lat_row, nw1, w_xbc, cw, cb, w_dt, dt_b, seq)
        z_l, f_l, uv_l, g_l = _inproj_b(hl, w_rest, tm_lat)
        y_l, _, _ = _ssd(xbc_l, dt_l, z_l, alog, dskip, snw, e3, s_f, s_b, seq, True)
        yf_l = _fft(f_l, seq)
        xl = _merge(y_l, yf_l, uv_l, g_l, xl, mod3, lat_row, w1, w2, w3, wo, ws, bs, seq, tm_lat)
        xl = _ffn(xl, mod3, lat_row, nw2, wu, fcw, fcb, wd, fw, seq, GRID_W, tm_lat,
                  final_norm=(i == depth - 1))
        if need_ctx:
            yf_c = _fft(f_c, cseq)
            xc = _merge(y_c, yf_c, uv_c, g_c, xc, mod3, ctx_row, w1, w2, w3, wo, ws, bs, cseq, tm_ctx)
            xc = _ffn(xc, mod3, ctx_row, nw2, wu, fcw, fcb, wd, fw, cseq, cseq, tm_ctx, final_norm=False)
    return xl.reshape(nb, seq, d)
```

```python
import functools
import math

import numpy as np
import jax
import jax.numpy as jnp
from jax import lax
from jax.experimental import pallas as pl
from jax.experimental.pallas import tpu as pltpu

F32 = jnp.float32
BF16 = jnp.bfloat16

D_MODEL = 1024
GRID_W = 64
SSD_HEADS = 16
SSD_HEAD_DIM = 64
SSD_INNER = SSD_HEADS * SSD_HEAD_DIM
SSD_GROUPS = 2
SSD_STATE = 128
SSD_CONV = 5
SSD_CHUNK = 128
SSD_XBC = SSD_INNER + 2 * SSD_GROUPS * SSD_STATE
SSD_COLS = SSD_XBC + 2 * SSD_HEADS
FFT_GROUPS = 4
FFT_GROUP_DIM = 128
FFT_WIDTH = FFT_GROUPS * FFT_GROUP_DIM
GMLP_GROUPS = 4
GMLP_GROUP_DIM = 128
GMLP_WIDTH = GMLP_GROUPS * GMLP_GROUP_DIM
GMLP_CHUNK = 128
N_BRANCH = 3
D_FF = 2816
FFN_CONV = 3
EPS = 1e-6

OFF_DT = SSD_XBC
OFF_Z = SSD_COLS
OFF_FFT = OFF_Z + SSD_INNER
OFF_GMLP = OFF_FFT + FFT_WIDTH
OFF_GATE = OFF_GMLP + 2 * GMLP_WIDTH

V7X_LANES = 128
V7X_SUBLANES = 8
V7X_VMEM_LIMIT_BYTES = 56 * 1024 * 1024

HEAD_REP = 3
HALO = V7X_SUBLANES
CONV_STRIDE = 4
CONV_GROUP = CONV_STRIDE * V7X_SUBLANES


def _dot(a, b):
    return jnp.dot(a, b, preferred_element_type=F32)


def _sigmoid(x):
    return 1.0 / (1.0 + jnp.exp(-x))


def _silu(x):
    return x * _sigmoid(x)


def _gelu_tanh(x):
    return 0.5 * x * (1.0 + jnp.tanh(math.sqrt(2.0 / math.pi) * (x + 0.044715 * (x * x * x))))


def _softplus(x):
    return jnp.maximum(x, 0.0) + jnp.log(1.0 + jnp.exp(-jnp.abs(x)))


def _params(n_grid):
    return pltpu.CompilerParams(dimension_semantics=("arbitrary",) * n_grid,
                                vmem_limit_bytes=V7X_VMEM_LIMIT_BYTES)


def _resident(shape):
    nd = len(shape)
    return pl.BlockSpec(shape, lambda *_: (0,) * nd, pipeline_mode=pl.Buffered(1))


def _mod_kernel(c_ref, w_ref, b_ref, o_ref):
    s = _silu(c_ref[...]).astype(BF16)
    o_ref[0] = _dot(s, w_ref[0].astype(BF16)) + b_ref[0]


def _modulation(cc, w_mod, b_mod):
    depth, d, n = w_mod.shape
    rows = cc.shape[0]
    tn = 1536
    return pl.pallas_call(
        _mod_kernel,
        out_shape=jax.ShapeDtypeStruct((depth, rows, n), F32),
        grid=(depth, n // tn),
        in_specs=[pl.BlockSpec((rows, d), lambda l, j: (0, 0)),
                  pl.BlockSpec((1, d, tn), lambda l, j: (l, 0, j)),
                  pl.BlockSpec((1, 1, tn), lambda l, j: (l, 0, j))],
        out_specs=pl.BlockSpec((1, rows, tn), lambda l, j: (l, 0, j)),
        compiler_params=_params(2),
        name="modulation",
    )(cc, w_mod, b_mod.reshape(depth, 1, n))


def _norm_modulate(x_ref, h_ref, nw, sh, sc1, n_rows, rc):
    def body(r, carry):
        rows = pl.ds(pl.multiple_of(r * rc, rc), rc)
        xf = x_ref[rows, :]
        ms = jnp.mean(xf * xf, axis=-1, keepdims=True)
        hn = (xf * lax.rsqrt(ms + EPS)) * nw
        h_ref[rows, :] = (hn * sc1 + sh).astype(BF16)
        return carry
    lax.fori_loop(0, n_rows // rc, body, 0)


def _conv_group(acc_ref, lead, base, g, taps, w, b, seg, n_rows):
    half = taps // 2
    v = {j: acc_ref[lead + (pl.ds(base + j, V7X_SUBLANES, stride=CONV_STRIDE), slice(None))]
         for j in range(-half, CONV_STRIDE + half)}
    sub = lax.broadcasted_iota(jnp.int32, (V7X_SUBLANES, V7X_LANES), 0)
    outs = []
    for j in range(CONV_STRIDE):
        out = None
        for k in range(taps):
            off = k - half
            term = v[j + off]
            if seg < n_rows and off != 0:
                for m in range(V7X_SUBLANES):
                    pos = (g * CONV_GROUP + j + CONV_STRIDE * m) % seg
                    if not 0 <= pos + off < seg:
                        term = jnp.where(sub == m, 0.0, term)
            term = term * w[k:k + 1, :]
            out = term if out is None else out + term
        outs.append(out + b)
    return outs


def _zero_halo(acc_ref, n_rows):
    lead = acc_ref.shape[:-2]
    z = jnp.zeros(lead + (HALO, acc_ref.shape[-1]), F32)
    idx = (slice(None),) * len(lead)
    acc_ref[idx + (slice(0, HALO), slice(None))] = z
    acc_ref[idx + (slice(HALO + n_rows, 2 * HALO + n_rows), slice(None))] = z


def _inproj_a_kernel(x_ref, sh_ref, sc_ref, nw_ref, w_ref, cw_ref, cb_ref, wdt_ref, dtb_ref,
                     h_ref, xbc_ref, dt_ref, acc_ref, tmp_ref, *, n_rows, rc, tn):
    _norm_modulate(x_ref, h_ref, nw_ref[...], sh_ref[0], 1.0 + sc_ref[0], n_rows, rc)
    _zero_halo(acc_ref, n_rows)
    n_slab = tn // V7X_LANES
    n_chunks = SSD_XBC // tn

    def dt_body(r, carry):
        rows = pl.ds(pl.multiple_of(r * rc, rc), rc)
        dt_ref[rows, :] = _softplus(_dot(h_ref[rows, :], wdt_ref[...]) + dtb_ref[...])
        return carry
    lax.fori_loop(0, n_rows // rc, dt_body, 0)

    for n in range(n_chunks + 1):
        def body(r, carry, n=n):
            row0 = pl.multiple_of(r * rc, rc)
            if n < n_chunks:
                res = _dot(h_ref[pl.ds(row0, rc), :], w_ref[:, n * tn:(n + 1) * tn])
                for s in range(n_slab):
                    acc_ref[n % 2, s, pl.ds(HALO + row0, rc), :] = res[:, s * V7X_LANES:(s + 1) * V7X_LANES]
            if n > 0:
                c0 = (n - 1) * tn
                for s in range(n_slab):
                    cols = slice(c0 + s * V7X_LANES, c0 + (s + 1) * V7X_LANES)
                    cw = cw_ref[:, cols]
                    cb = cb_ref[:, cols]
                    for g in range(rc // CONV_GROUP):
                        outs = _conv_group(acc_ref, ((n - 1) % 2, s), HALO + row0 + g * CONV_GROUP, g,
                                           SSD_CONV, cw, cb, n_rows, n_rows)
                        for j, u in enumerate(outs):
                            tmp_ref[s, pl.ds(g * CONV_GROUP + j, V7X_SUBLANES, stride=CONV_STRIDE), :] = _silu(u)
                    xbc_ref[pl.ds(row0, rc), cols] = tmp_ref[s].astype(BF16)
            return carry
        lax.fori_loop(0, n_rows // rc, body, 0)


def _inproj_a(x2, mod3, mod_row, nw, w_xbc, cw, cb, w_dt, dt_b, seq):
    nt, d = x2.shape
    nb = nt // seq
    tn = 512
    rc = min(256, seq)
    ndt = w_dt.shape[1]
    n_slab = tn // V7X_LANES
    kern = functools.partial(_inproj_a_kernel, n_rows=seq, rc=rc, tn=tn)
    return pl.pallas_call(
        kern,
        out_shape=(jax.ShapeDtypeStruct((nt, d), BF16),
                   jax.ShapeDtypeStruct((nt, SSD_XBC), BF16),
                   jax.ShapeDtypeStruct((nt, ndt), F32)),
        grid=(nb,),
        in_specs=[pl.BlockSpec((seq, d), lambda i: (i, 0)),
                  pl.BlockSpec((1, 1, d), lambda i: (mod_row(i), 0, 0)),
                  pl.BlockSpec((1, 1, d), lambda i: (mod_row(i), 0, 1)),
                  _resident((1, d)),
                  _resident((d, SSD_XBC)),
                  _resident(cw.shape),
                  _resident((1, SSD_XBC)),
                  _resident((d, ndt)),
                  _resident((1, ndt))],
        out_specs=(pl.BlockSpec((seq, d), lambda i: (i, 0)),
                   pl.BlockSpec((seq, SSD_XBC), lambda i: (i, 0)),
                   pl.BlockSpec((seq, ndt), lambda i: (i, 0))),
        scratch_shapes=[pltpu.VMEM((2, n_slab, seq + 2 * HALO, V7X_LANES), F32),
                        pltpu.VMEM((n_slab, rc, V7X_LANES), F32)],
        compiler_params=_params(1),
        name="inproj_a",
    )(x2, mod3, mod3, nw, w_xbc, cw, cb, w_dt, dt_b)


_REST_Z = 0
_REST_F = SSD_INNER
_REST_UV = _REST_F + FFT_WIDTH
_REST_GATE = _REST_UV + 2 * GMLP_WIDTH
_REST_COLS = _REST_GATE + N_BRANCH * D_MODEL


def _inproj_b_kernel(h_ref, w_ref, z_ref, f_ref, uv_ref, g_ref, *, tn):
    h = h_ref[...]
    for n in range(_REST_COLS // tn):
        c0 = n * tn
        r = _dot(h, w_ref[:, c0:c0 + tn])
        if c0 < _REST_F:
            z_ref[:, c0:c0 + tn] = _silu(r).astype(BF16)
        elif c0 < _REST_UV:
            f_ref[:, c0 - _REST_F:c0 - _REST_F + tn] = r.astype(BF16)
        elif c0 < _REST_GATE:
            uv_ref[:, c0 - _REST_UV:c0 - _REST_UV + tn] = _gelu_tanh(r).astype(BF16)
        else:
            g_ref[:, c0 - _REST_GATE:c0 - _REST_GATE + tn] = _sigmoid(r).astype(BF16)


def _inproj_b(h2, w_rest, tm):
    nt, d = h2.shape
    tn = 512
    widths = (SSD_INNER, FFT_WIDTH, 2 * GMLP_WIDTH, N_BRANCH * D_MODEL)
    return pl.pallas_call(
        functools.partial(_inproj_b_kernel, tn=tn),
        out_shape=tuple(jax.ShapeDtypeStruct((nt, w), BF16) for w in widths),
        grid=(nt // tm,),
        in_specs=[pl.BlockSpec((tm, d), lambda i: (i, 0)), _resident((d, _REST_COLS))],
        out_specs=tuple(pl.BlockSpec((tm, w), lambda i: (i, 0)) for w in widths),
        compiler_params=_params(1),
        name="inproj_b",
    )(h2, w_rest)


def _split3_pack(v, lane):
    hi = v.astype(BF16).astype(F32)
    r1 = v - hi
    mid = r1.astype(BF16).astype(F32)
    lo = r1 - mid
    return jnp.where(lane < SSD_HEADS, hi, jnp.where(lane < 2 * SSD_HEADS, mid, lo)).astype(BF16)


def _split3_stack(v):
    hi = v.astype(BF16)
    r1 = v - hi.astype(F32)
    mid = r1.astype(BF16)
    lo = (r1 - mid.astype(F32)).astype(BF16)
    return jnp.concatenate([hi, mid, lo], axis=0)


def _ssd_kernel(xbc_ref, dt_ref, z_ref, alog_ref, dskip_ref, nw_ref, e_ref, if_ref, ib_ref,
                *refs, n_rows, with_y):
    if with_y:
        y_ref, ff_ref, fb_ref, yacc_ref, ct_ref, dtt_ref, st_ref = refs
    else:
        ff_ref, fb_ref, st_ref = refs
        y_ref = yacc_ref = ct_ref = dtt_ref = None
    q = SSD_CHUNK
    nc = n_rows // q
    gw = SSD_INNER // SSD_GROUPS
    hpg = SSD_HEADS // SSD_GROUPS

    ri = lax.broadcasted_iota(jnp.int32, (q, q), 0)
    ci = lax.broadcasted_iota(jnp.int32, (q, q), 1)
    lane = ci
    low_mask = ci <= ri
    up_mask = ci >= ri
    tri_low = jnp.where(low_mask, 1.0, 0.0).astype(BF16)
    tri_up = jnp.where(up_mask, 1.0, 0.0).astype(BF16)
    tri3 = (jnp.concatenate([tri_low] * 3, axis=1), jnp.concatenate([tri_up] * 3, axis=1))
    masks = (low_mask, up_mask)
    first_half = lane < SSD_HEAD_DIM
    lane8 = lax.broadcasted_iota(jnp.int32, (V7X_SUBLANES, q), 1)

    if with_y:
        yacc_ref[...] = jnp.zeros(yacc_ref.shape, F32)

    for d in range(2):
        st_ref[...] = (if_ref if d == 0 else ib_ref)[0]
        a_row = -jnp.exp(alog_ref[d:d + 1, :])
        mask = masks[d]

        def chunk(k, carry, d=d, a_row=a_row, mask=mask):
            c = k if d == 0 else nc - 1 - k
            rows = pl.ds(pl.multiple_of(c * q, q), q)
            dt_c = dt_ref[rows, d * q:(d + 1) * q]
            la = dt_c * a_row
            cum = _dot(tri3[d], _split3_stack(la))
            tot = cum[q - 1:q, :] if d == 0 else cum[0:1, :]
            w = dt_c * jnp.exp(tot - cum)
            w_x = _dot(_split3_pack(w, lane), e_ref[...])
            etot = jnp.broadcast_to(jnp.exp(tot), (V7X_SUBLANES, q))
            etot_x = _dot(_split3_pack(etot, lane8), e_ref[...])[0:1, :]
            xs = xbc_ref[rows, 0:SSD_INNER]
            wx = xs * w_x.astype(BF16)
            if with_y:
                ct_ref[...] = cum.T
                dtt_ref[...] = dt_c.T
            for g in range(SSD_GROUPS):
                bm = xbc_ref[rows, SSD_INNER + g * SSD_STATE:SSD_INNER + (g + 1) * SSD_STATE]
                cm = xbc_ref[rows, SSD_INNER + (SSD_GROUPS + g) * SSD_STATE:
                             SSD_INNER + (SSD_GROUPS + g + 1) * SSD_STATE]
                st = st_ref[g]
                if with_y:
                    cb = lax.dot_general(cm, bm, (((1,), (1,)), ((), ())), preferred_element_type=F32)
                    cm_f = cm.astype(F32)
                    st_b = st.astype(BF16)
                    for j in range(hpg // 2):
                        pair = g * (hpg // 2) + j
                        rhs = jnp.concatenate(
                            [xs[:, pair * q:(pair + 1) * q], st_b[:, j * q:(j + 1) * q]], axis=0)
                        ys = []
                        for t in range(2):
                            h = 2 * pair + t
                            colb = jnp.broadcast_to(cum[:, h:h + 1], (q, q))
                            diff = jnp.minimum(colb - ct_ref[h:h + 1, :], 0.0)
                            lm = jnp.where(mask, jnp.exp(diff), 0.0)
                            m_h = (cb * lm) * dtt_ref[h:h + 1, :]
                            cm_s = cm_f * jnp.exp(colb)
                            lhs = jnp.concatenate([m_h.astype(BF16), cm_s.astype(BF16)], axis=1)
                            ys.append(_dot(lhs, rhs))
                        y_pair = jnp.where(first_half, ys[0], ys[1])
                        yacc_ref[rows, pair * q:(pair + 1) * q] += y_pair
                s_new = lax.dot_general(bm, wx[:, g * gw:(g + 1) * gw], (((0,), (0,)), ((), ())),
                                        preferred_element_type=F32)
                st_ref[g] = st * etot_x[:, g * gw:(g + 1) * gw] + s_new
            return carry

        lax.fori_loop(0, nc, chunk, 0)
        (ff_ref if d == 0 else fb_ref)[0] = st_ref[...]

    if with_y:
        dsk = dskip_ref[...]
        nw = nw_ref[...]

        def fin(c, carry):
            rows = pl.ds(pl.multiple_of(c * q, q), q)
            xs = xbc_ref[rows, 0:SSD_INNER].astype(F32)
            y = (yacc_ref[rows, :] + dsk * xs) * z_ref[rows, :].astype(F32)
            for g in range(SSD_GROUPS):
                yg = y[:, g * gw:(g + 1) * gw]
                ms = jnp.mean(yg * yg, axis=-1, keepdims=True)
                y_ref[rows, g * gw:(g + 1) * gw] = (
                    (yg * lax.rsqrt(ms + EPS)) * nw[:, g * gw:(g + 1) * gw]).astype(BF16)
            return carry
        lax.fori_loop(0, nc, fin, 0)


def _ssd(xbc, dt, zs, alog, dskip, nw, e3, init_f, init_b, seq, with_y):
    nt = xbc.shape[0]
    nb = nt // seq
    ndt = dt.shape[1]
    st_shape = (SSD_GROUPS, SSD_STATE, SSD_INNER // SSD_GROUPS)
    st_spec = pl.BlockSpec((1,) + st_shape, lambda i: (i, 0, 0, 0))
    st_sds = jax.ShapeDtypeStruct((nb,) + st_shape, F32)
    in_specs = [pl.BlockSpec((seq, SSD_XBC), lambda i: (i, 0)),
                pl.BlockSpec((seq, ndt), lambda i: (i, 0)),
                pl.BlockSpec((seq, SSD_INNER), lambda i: (i, 0)),
                _resident(alog.shape), _resident((1, SSD_INNER)), _resident((1, SSD_INNER)),
                _resident(e3.shape), st_spec, st_spec]
    out_shape = [st_sds, st_sds]
    out_specs = [st_spec, st_spec]
    scratch = [pltpu.VMEM(st_shape, F32)]
    if with_y:
        out_shape = [jax.ShapeDtypeStruct((nt, SSD_INNER), BF16)] + out_shape
        out_specs = [pl.BlockSpec((seq, SSD_INNER), lambda i: (i, 0))] + out_specs
        scratch = [pltpu.VMEM((seq, SSD_INNER), F32),
                   pltpu.VMEM((SSD_CHUNK, V7X_LANES), F32),
                   pltpu.VMEM((SSD_CHUNK, V7X_LANES), F32)] + scratch
    return pl.pallas_call(
        functools.partial(_ssd_kernel, n_rows=seq, with_y=with_y),
        out_shape=tuple(out_shape),
        grid=(nb,),
        in_specs=in_specs,
        out_specs=tuple(out_specs),
        scratch_shapes=scratch,
        compiler_params=_params(1),
        name="ssd_y" if with_y else "ssd_state",
    )(xbc, dt, zs, alog, dskip, nw, e3, init_f, init_b)


def _fft_kernel(f_ref, cs_ref, dl_ref, o_ref, r_ref, *, n_rows):
    gd = FFT_GROUP_DIM
    mb = min(256, n_rows)

    def stage1(r, carry):
        rows = pl.ds(pl.multiple_of(r * mb, mb), mb)
        for g in range(FFT_GROUPS):
            t = _dot(f_ref[rows, g * gd:(g + 1) * gd], cs_ref[...])
            r_ref[rows, g * gd:(g + 1) * gd] = t[:, :gd].astype(BF16)
            r_ref[pl.ds(pl.multiple_of(n_rows + r * mb, mb), mb), g * gd:(g + 1) * gd] = t[:, gd:].astype(BF16)
        return carry
    lax.fori_loop(0, n_rows // mb, stage1, 0)

    def stage2(r, carry):
        rows = pl.ds(pl.multiple_of(r * mb, mb), mb)
        o_ref[rows, :] = _dot(dl_ref[rows, :], r_ref[...]).astype(BF16)
        return carry
    lax.fori_loop(0, n_rows // mb, stage2, 0)


def _dft_constants(seq):
    k = np.arange(seq, dtype=np.float64)
    ang = 2.0 * np.pi * np.outer(k, k) / seq
    dl = np.concatenate([np.cos(ang), -np.sin(ang)], axis=1) / math.sqrt(seq)
    c = np.arange(FFT_GROUP_DIM, dtype=np.float64)
    angc = 2.0 * np.pi * np.outer(c, c) / FFT_GROUP_DIM
    cs = np.concatenate([np.cos(angc), np.sin(angc)], axis=1) / math.sqrt(FFT_GROUP_DIM)
    return jnp.asarray(cs, BF16), jnp.asarray(dl, BF16)


def _fft(f2, seq):
    nt = f2.shape[0]
    cs, dl = _dft_constants(seq)
    return pl.pallas_call(
        functools.partial(_fft_kernel, n_rows=seq),
        out_shape=jax.ShapeDtypeStruct((nt, FFT_WIDTH), BF16),
        grid=(nt // seq,),
        in_specs=[pl.BlockSpec((seq, FFT_WIDTH), lambda i: (i, 0)),
                  _resident(cs.shape), _resident(dl.shape)],
        out_specs=pl.BlockSpec((seq, FFT_WIDTH), lambda i: (i, 0)),
        scratch_shapes=[pltpu.VMEM((2 * seq, FFT_WIDTH), BF16)],
        compiler_params=_params(1),
        name="fourier_mix",
    )(f2, cs, dl)


def _merge_kernel(ys_ref, yf_ref, uv_ref, g_ref, x_ref, gm_ref, w1_ref, w2_ref, w3_ref, wo_ref,
                  ws_ref, bs_ref, o_ref, yg_ref, m_ref, *, tm):
    gd = GMLP_GROUP_DIM
    for c in range(tm // GMLP_CHUNK):
        rows = slice(c * GMLP_CHUNK, (c + 1) * GMLP_CHUNK)
        for g in range(GMLP_GROUPS):
            v = uv_ref[rows, GMLP_WIDTH + g * gd:GMLP_WIDTH + (g + 1) * gd]
            s = _dot(ws_ref[g], v) + bs_ref[g]
            u = uv_ref[rows, g * gd:(g + 1) * gd].astype(F32)
            yg_ref[rows, g * gd:(g + 1) * gd] = (u * s).astype(BF16)
    tn = 512
    for n in range(D_MODEL // tn):
        cols = slice(n * tn, (n + 1) * tn)
        m = g_ref[:, n * tn:(n + 1) * tn].astype(F32) * _dot(ys_ref[...], w1_ref[:, cols])
        m += g_ref[:, D_MODEL + n * tn:D_MODEL + (n + 1) * tn].astype(F32) * _dot(yf_ref[...], w2_ref[:, cols])
        m += g_ref[:, 2 * D_MODEL + n * tn:2 * D_MODEL + (n + 1) * tn].astype(F32) * _dot(yg_ref[...], w3_ref[:, cols])
        m_ref[:, cols] = m.astype(BF16)
    o_ref[...] = x_ref[...] + gm_ref[0] * _dot(m_ref[...], wo_ref[...])


def _merge(ys, yf, uv, gate, x2, mod3, mod_row, w1, w2, w3, wo, ws, bs, seq, tm):
    nt, d = x2.shape
    per_seq = seq // tm
    row = lambda i: (i, 0)
    return pl.pallas_call(
        functools.partial(_merge_kernel, tm=tm),
        out_shape=jax.ShapeDtypeStruct((nt, d), F32),
        grid=(nt // tm,),
        in_specs=[pl.BlockSpec((tm, SSD_INNER), row),
                  pl.BlockSpec((tm, FFT_WIDTH), row),
                  pl.BlockSpec((tm, 2 * GMLP_WIDTH), row),
                  pl.BlockSpec((tm, N_BRANCH * d), row),
                  pl.BlockSpec((tm, d), row),
                  pl.BlockSpec((1, 1, d), lambda i: (mod_row(i // per_seq), 0, 2)),
                  _resident(w1.shape), _resident(w2.shape), _resident(w3.shape), _resident(wo.shape),
                  _resident(ws.shape), _resident(bs.shape)],
        out_specs=pl.BlockSpec((tm, d), row),
        scratch_shapes=[pltpu.VMEM((tm, GMLP_WIDTH), BF16), pltpu.VMEM((tm, d), BF16)],
        compiler_params=_params(1),
        name="merge",
    )(ys, yf, uv, gate, x2, mod3, w1, w2, w3, wo, ws, bs)


def _ffn_kernel(x_ref, sh_ref, sc_ref, gm_ref, nw_ref, wu_ref, cw_ref, cb_ref, wd_ref, fw_ref,
                o_ref, h_ref, acc_ref, tmp_ref, act_ref, *, tm, tf, seg, final_norm):
    rc = min(256, tm)
    _norm_modulate(x_ref, h_ref, nw_ref[...], sh_ref[0], 1.0 + sc_ref[0], tm, rc)
    _zero_halo(acc_ref, tm)
    n_slab = tf // V7X_LANES
    n_chunks = D_FF // tf
    h = h_ref[...]
    for c in range(n_chunks + 1):
        if c < n_chunks:
            for t, c0 in enumerate((c * tf, D_FF + c * tf)):
                res = _dot(h, wu_ref[:, c0:c0 + tf])
                for s in range(n_slab):
                    acc_ref[c % 2, t, s, HALO:HALO + tm, :] = res[:, s * V7X_LANES:(s + 1) * V7X_LANES]
        if c > 0:
            p = c - 1
            for s in range(n_slab):
                ca = slice(p * tf + s * V7X_LANES, p * tf + (s + 1) * V7X_LANES)
                cv = slice(D_FF + p * tf + s * V7X_LANES, D_FF + p * tf + (s + 1) * V7X_LANES)
                cwa, cba, cwv, cbv = cw_ref[:, ca], cb_ref[:, ca], cw_ref[:, cv], cb_ref[:, cv]
                for g in range(tm // CONV_GROUP):
                    base = HALO + g * CONV_GROUP
                    a = _conv_group(acc_ref, (p % 2, 0, s), base, g, FFN_CONV, cwa, cba, seg, tm)
                    v = _conv_group(acc_ref, (p % 2, 1, s), base, g, FFN_CONV, cwv, cbv, seg, tm)
                    for j in range(CONV_STRIDE):
                        tmp_ref[s, pl.ds(g * CONV_GROUP + j, V7X_SUBLANES, stride=CONV_STRIDE), :] = _silu(a[j]) * v[j]
                act_ref[:, ca] = tmp_ref[s].astype(BF16)
    y = x_ref[...] + gm_ref[0] * _dot(act_ref[...], wd_ref[...])
    if final_norm:
        ms = jnp.mean(y * y, axis=-1, keepdims=True)
        y = (y * lax.rsqrt(ms + EPS)) * fw_ref[...]
    o_ref[...] = y


def _ffn(x2, mod3, mod_row, nw, wu, cw, cb, wd, fw, seq, seg, tm, final_norm):
    nt, d = x2.shape
    per_seq = seq // tm
    tf = 256
    n_slab = tf // V7X_LANES
    mrow = lambda k: (lambda i: (mod_row(i // per_seq), 0, k))
    return pl.pallas_call(
        functools.partial(_ffn_kernel, tm=tm, tf=tf, seg=seg, final_norm=final_norm),
        out_shape=jax.ShapeDtypeStruct((nt, d), F32),
        grid=(nt // tm,),
        in_specs=[pl.BlockSpec((tm, d), lambda i: (i, 0)),
                  pl.BlockSpec((1, 1, d), mrow(3)),
                  pl.BlockSpec((1, 1, d), mrow(4)),
                  pl.BlockSpec((1, 1, d), mrow(5)),
                  _resident((1, d)), _resident(wu.shape), _resident(cw.shape), _resident(cb.shape),
                  _resident(wd.shape), _resident((1, d))],
        out_specs=pl.BlockSpec((tm, d), lambda i: (i, 0)),
        scratch_shapes=[pltpu.VMEM((tm, d), BF16),
                        pltpu.VMEM((2, 2, n_slab, tm + 2 * HALO, V7X_LANES), F32),
                        pltpu.VMEM((n_slab, tm, V7X_LANES), F32),
                        pltpu.VMEM((tm, D_FF), BF16)],
        compiler_params=_params(1),
        name="conv_ffn",
    )(x2, mod3, mod3, mod3, nw, wu, cw, cb, wd, fw)


def _rep_heads(v):
    pad = V7X_LANES - HEAD_REP * SSD_HEADS
    rep = jnp.concatenate([v] * HEAD_REP, axis=-1)
    return jnp.pad(rep, [(0, 0)] * (v.ndim - 1) + [(0, pad)])


def _head_expand_matrix():
    e = np.zeros((V7X_LANES, SSD_INNER), np.float32)
    for k in range(HEAD_REP * SSD_HEADS):
        h = k % SSD_HEADS
        e[k, h * SSD_HEAD_DIM:(h + 1) * SSD_HEAD_DIM] = 1.0
    return jnp.asarray(e, BF16)


def _pad_rows(a, rows):
    return jnp.pad(a, ((0, rows - a.shape[0]), (0, 0)))


def kernel(x, c, ctx, c_ctx, w_mod, b_mod, norm1_w, w_in, ssd_conv_w, ssd_conv_b, ssd_a_log,
           ssd_dt_bias, ssd_d, ssd_norm_w, gmlp_w_s, gmlp_b_s, w_ssd_o, w_fft_o, w_gmlp_o, w_out,
           norm2_w, ffn_w_up, ffn_conv_w, ffn_conv_b, ffn_w_down, final_norm_w):
    nb, seq, d = x.shape
    cseq = ctx.shape[1]
    depth = w_mod.shape[0]
    mod_rows = -(-(nb + 1) // V7X_SUBLANES) * V7X_SUBLANES
    cc = jnp.concatenate([c, c_ctx[None, :], jnp.zeros((mod_rows - nb - 1, d), F32)], axis=0)
    mod = _modulation(cc, w_mod, b_mod)
    lat_row = lambda i: i
    ctx_row = lambda i: nb
    e3 = _head_expand_matrix()
    tm_lat = 512
    tm_ctx = cseq

    xl = x.reshape(nb * seq, d)
    xc = ctx.reshape(nb * cseq, d)
    for i in range(depth):
        need_ctx = i < depth - 1
        mod3 = mod[i].reshape(mod_rows, 1, 6 * d)
        wi = w_in[i]
        w_xbc = wi[:, :SSD_XBC].astype(BF16)
        w_dt = jnp.concatenate([_rep_heads(wi[:, OFF_DT:OFF_DT + SSD_HEADS]),
                                _rep_heads(wi[:, OFF_DT + SSD_HEADS:OFF_Z])], axis=1).astype(BF16)
        dt_b = jnp.concatenate([_rep_heads(ssd_dt_bias[i, 0][None]), _rep_heads(ssd_dt_bias[i, 1][None])], axis=1)
        w_rest = wi[:, OFF_Z:].astype(BF16)
        cw = _pad_rows(ssd_conv_w[i], V7X_SUBLANES)
        cb = ssd_conv_b[i][None, :]
        alog = _pad_rows(_rep_heads(ssd_a_log[i]), V7X_SUBLANES)
        dskip = jnp.repeat(ssd_d[i], SSD_HEAD_DIM)[None, :]
        snw = ssd_norm_w[i][None, :]
        nw1 = norm1_w[i][None, :]
        nw2 = norm2_w[i][None, :]
        w1 = w_ssd_o[i].astype(BF16)
        w2 = w_fft_o[i].astype(BF16)
        w3 = w_gmlp_o[i].astype(BF16)
        wo = w_out[i].astype(BF16)
        ws = gmlp_w_s[i].astype(BF16)
        bs = jnp.broadcast_to(gmlp_b_s[i][:, :, None], (GMLP_GROUPS, GMLP_CHUNK, GMLP_GROUP_DIM))
        wu = ffn_w_up[i].astype(BF16)
        fcw = _pad_rows(ffn_conv_w[i], V7X_SUBLANES)
        fcb = ffn_conv_b[i][None, :]
        wd = ffn_w_down[i].astype(BF16)
        fw = final_norm_w[None, :]

        hc, xbc_c, dt_c = _inproj_a(xc, mod3, ctx_row, nw1, w_xbc, cw, cb, w_dt, dt_b, cseq)
        zero_state = jnp.zeros((nb, SSD_GROUPS, SSD_STATE, SSD_INNER // SSD_GROUPS), F32)
        if need_ctx:
            z_c, f_c, uv_c, g_c = _inproj_b(hc, w_rest, tm_ctx)
            y_c, s_f, s_b = _ssd(xbc_c, dt_c, z_c, alog, dskip, snw, e3, zero_state, zero_state, cseq, True)
        else:
            s_f, s_b = _ssd(xbc_c, dt_c, xbc_c[:, :SSD_INNER], alog, dskip, snw, e3, zero_state, zero_state,
                            cseq, False)

        hl, xbc_l, dt_l = _inproj_a(xl, mod3, lat_row, nw1, w_xbc, cw, cb, w_dt, dt_b, seq)
        z_l, f_l, uv_l, g_l = _inproj_b(hl, w_rest, tm_lat)
        y_l, _, _ = _ssd(xbc_l, dt_l, z_l, alog, dskip, snw, e3, s_f, s_b, seq, True)
        yf_l = _fft(f_l, seq)
        xl = _merge(y_l, yf_l, uv_l, g_l, xl, mod3, lat_row, w1, w2, w3, wo, ws, bs, seq, tm_lat)
        xl = _ffn(xl, mod3, lat_row, nw2, wu, fcw, fcb, wd, fw, seq, GRID_W, tm_lat,
                  final_norm=(i == depth - 1))
        if need_ctx:
            yf_c = _fft(f_c, cseq)
            xc = _merge(y_c, yf_c, uv_c, g_c, xc, mod3, ctx_row, w1, w2, w3, wo, ws, bs, cseq, tm_ctx)
            xc = _ffn(xc, mod3, ctx_row, nw2, wu, fcw, fcb, wd, fw, cseq, cseq, tm_ctx, final_norm=False)
    return xl.reshape(nb, seq, d)
```

```python
import functools
import math

import numpy as np
import jax
import jax.numpy as jnp
from jax import lax
from jax.experimental import pallas as pl
from jax.experimental.pallas import tpu as pltpu

F32 = jnp.float32
BF16 = jnp.bfloat16

D_MODEL = 1024
GRID_W = 64
SSD_HEADS = 16
SSD_HEAD_DIM = 64
SSD_INNER = SSD_HEADS * SSD_HEAD_DIM
SSD_GROUPS = 2
SSD_STATE = 128
SSD_CONV = 5
SSD_CHUNK = 128
SSD_XBC = SSD_INNER + 2 * SSD_GROUPS * SSD_STATE
SSD_COLS = SSD_XBC + 2 * SSD_HEADS
FFT_GROUPS = 4
FFT_GROUP_DIM = 128
FFT_WIDTH = FFT_GROUPS * FFT_GROUP_DIM
GMLP_GROUPS = 4
GMLP_GROUP_DIM = 128
GMLP_WIDTH = GMLP_GROUPS * GMLP_GROUP_DIM
GMLP_CHUNK = 128
N_BRANCH = 3
D_FF = 2816
FFN_CONV = 3
EPS = 1e-6

OFF_DT = SSD_XBC
OFF_Z = SSD_COLS
OFF_FFT = OFF_Z + SSD_INNER
OFF_GMLP = OFF_FFT + FFT_WIDTH
OFF_GATE = OFF_GMLP + 2 * GMLP_WIDTH

V7X_LANES = 128
V7X_SUBLANES = 8
V7X_VMEM_LIMIT_BYTES = 56 * 1024 * 1024

NEG_BIG = -1e30
HEAD_REP = 3
HALO = V7X_SUBLANES
CONV_STRIDE = 4
CONV_GROUP = CONV_STRIDE * V7X_SUBLANES


def _dot(a, b):
    return jnp.dot(a, b, preferred_element_type=F32)


def _sigmoid(x):
    return 1.0 / (1.0 + jnp.exp(-x))


def _silu(x):
    hx = 0.5 * x
    return hx + hx * jnp.tanh(hx)


def _gelu_tanh(x):
    return 0.5 * x * (1.0 + jnp.tanh(math.sqrt(2.0 / math.pi) * (x + 0.044715 * (x * x * x))))


def _softplus(x):
    return jnp.maximum(x, 0.0) + jnp.log(1.0 + jnp.exp(-jnp.abs(x)))


def _params(n_grid):
    return pltpu.CompilerParams(dimension_semantics=("arbitrary",) * n_grid,
                                vmem_limit_bytes=V7X_VMEM_LIMIT_BYTES)


def _resident(shape):
    nd = len(shape)
    return pl.BlockSpec(shape, lambda *_: (0,) * nd, pipeline_mode=pl.Buffered(1))


def _mod_kernel(c_ref, w_ref, b_ref, o_ref):
    s = _silu(c_ref[...]).astype(BF16)
    o_ref[0] = _dot(s, w_ref[0].astype(BF16)) + b_ref[0]


def _modulation(cc, w_mod, b_mod):
    depth, d, n = w_mod.shape
    rows = cc.shape[0]
    tn = 1536
    return pl.pallas_call(
        _mod_kernel,
        out_shape=jax.ShapeDtypeStruct((depth, rows, n), F32),
        grid=(depth, n // tn),
        in_specs=[pl.BlockSpec((rows, d), lambda l, j: (0, 0)),
                  pl.BlockSpec((1, d, tn), lambda l, j: (l, 0, j)),
                  pl.BlockSpec((1, 1, tn), lambda l, j: (l, 0, j))],
        out_specs=pl.BlockSpec((1, rows, tn), lambda l, j: (l, 0, j)),
        compiler_params=_params(2),
        name="modulation",
    )(cc, w_mod, b_mod.reshape(depth, 1, n))


def _norm_modulate(x_ref, h_ref, nw, sh, sc1, n_rows, rc):
    def body(r, carry):
        rows = pl.ds(pl.multiple_of(r * rc, rc), rc)
        xf = x_ref[rows, :]
        ms = jnp.mean(xf * xf, axis=-1, keepdims=True)
        hn = (xf * lax.rsqrt(ms + EPS)) * nw
        h_ref[rows, :] = (hn * sc1 + sh).astype(BF16)
        return carry
    lax.fori_loop(0, n_rows // rc, body, 0)


def _conv_group(acc_ref, lead, base, g, taps, w, b, seg, n_rows):
    half = taps // 2
    v = {j: acc_ref[lead + (pl.ds(base + j, V7X_SUBLANES, stride=CONV_STRIDE), slice(None))]
         for j in range(-half, CONV_STRIDE + half)}
    sub = lax.broadcasted_iota(jnp.int32, (V7X_SUBLANES, V7X_LANES), 0)
    outs = []
    for j in range(CONV_STRIDE):
        out = None
        for k in range(taps):
            off = k - half
            term = v[j + off]
            if seg < n_rows and off != 0:
                for m in range(V7X_SUBLANES):
                    pos = (g * CONV_GROUP + j + CONV_STRIDE * m) % seg
                    if not 0 <= pos + off < seg:
                        term = jnp.where(sub == m, 0.0, term)
            term = term * w[k:k + 1, :]
            out = term if out is None else out + term
        outs.append(out + b)
    return outs


def _zero_halo(acc_ref, n_rows):
    lead = acc_ref.shape[:-2]
    z = jnp.zeros(lead + (HALO, acc_ref.shape[-1]), F32)
    idx = (slice(None),) * len(lead)
    acc_ref[idx + (slice(0, HALO), slice(None))] = z
    acc_ref[idx + (slice(HALO + n_rows, 2 * HALO + n_rows), slice(None))] = z


A_HALO = 16


def _inproj_a_kernel(x_ref, xp_ref, xn_ref, sh_ref, sc_ref, nw_ref, w_ref, cw_ref, cb_ref, wdt_ref, dtb_ref,
                     h_ref, xbc_ref, dt_ref, hh_ref, acc_ref, tmp_ref, *, tm, n_blk, tn):
    nw = nw_ref[...]
    sh = sh_ref[0]
    sc1 = 1.0 + sc_ref[0]

    def norm_mod(xf):
        ms = jnp.mean(xf * xf, axis=-1, keepdims=True)
        return (((xf * lax.rsqrt(ms + EPS)) * nw) * sc1 + sh).astype(BF16)

    rc = min(256, tm)
    for r in range(tm // rc):
        h_ref[r * rc:(r + 1) * rc, :] = norm_mod(x_ref[r * rc:(r + 1) * rc, :])
    hh_ref[0:A_HALO, :] = norm_mod(xp_ref[...])
    hh_ref[A_HALO:2 * A_HALO, :] = norm_mod(xn_ref[...])
    h = h_ref[...]
    dt_ref[...] = _softplus(_dot(h, wdt_ref[...]) + dtb_ref[...])

    blk = pl.program_id(0) % n_blk
    n_slab = tn // V7X_LANES
    zero = jnp.zeros((HALO, V7X_LANES), F32)
    for n in range(SSD_XBC // tn + 1):
        if n < SSD_XBC // tn:
            wn = w_ref[:, n * tn:(n + 1) * tn]
            res = _dot(h, wn)
            halo = _dot(hh_ref[...], wn)
            for s in range(n_slab):
                lanes = slice(s * V7X_LANES, (s + 1) * V7X_LANES)
                acc_ref[n % 2, s, HALO:HALO + tm, :] = res[:, lanes]
                acc_ref[n % 2, s, 0:HALO, :] = jnp.where(blk == 0, zero, halo[A_HALO - HALO:A_HALO, lanes])
                acc_ref[n % 2, s, HALO + tm:2 * HALO + tm, :] = jnp.where(
                    blk == n_blk - 1, zero, halo[A_HALO:A_HALO + HALO, lanes])
        if n > 0:
            c0 = (n - 1) * tn
            for s in range(n_slab):
                cols = slice(c0 + s * V7X_LANES, c0 + (s + 1) * V7X_LANES)
                cw = cw_ref[:, cols]
                cb = cb_ref[:, cols]
                for g in range(tm // CONV_GROUP):
                    outs = _conv_group(acc_ref, ((n - 1) % 2, s), HALO + g * CONV_GROUP, g,
                                       SSD_CONV, cw, cb, tm, tm)
                    for j, u in enumerate(outs):
                        tmp_ref[s, pl.ds(g * CONV_GROUP + j, V7X_SUBLANES, stride=CONV_STRIDE), :] = _silu(u)
                xbc_ref[:, cols] = tmp_ref[s].astype(BF16)


def _inproj_a(x2, mod3, mod_row, nw, w_xbc, cw, cb, w_dt, dt_b, seq):
    nt, d = x2.shape
    tn = 512
    tm = min(512, seq)
    n_blk = seq // tm
    ndt = w_dt.shape[1]
    n_slab = tn // V7X_LANES
    per = tm // A_HALO
    last = nt // A_HALO - 1
    kern = functools.partial(_inproj_a_kernel, tm=tm, n_blk=n_blk, tn=tn)
    return pl.pallas_call(
        kern,
        out_shape=(jax.ShapeDtypeStruct((nt, d), BF16),
                   jax.ShapeDtypeStruct((nt, SSD_XBC), BF16),
                   jax.ShapeDtypeStruct((nt, ndt), F32)),
        grid=(nt // tm,),
        in_specs=[pl.BlockSpec((tm, d), lambda i: (i, 0)),
                  pl.BlockSpec((A_HALO, d), lambda i: (jnp.maximum(i * per - 1, 0), 0)),
                  pl.BlockSpec((A_HALO, d), lambda i: (jnp.minimum((i + 1) * per, last), 0)),
                  pl.BlockSpec((1, 1, d), lambda i: (mod_row(i // n_blk), 0, 0)),
                  pl.BlockSpec((1, 1, d), lambda i: (mod_row(i // n_blk), 0, 1)),
                  _resident((1, d)),
                  _resident((d, SSD_XBC)),
                  _resident(cw.shape),
                  _resident((1, SSD_XBC)),
                  _resident((d, ndt)),
                  _resident((1, ndt))],
        out_specs=(pl.BlockSpec((tm, d), lambda i: (i, 0)),
                   pl.BlockSpec((tm, SSD_XBC), lambda i: (i, 0)),
                   pl.BlockSpec((tm, ndt), lambda i: (i, 0))),
        scratch_shapes=[pltpu.VMEM((2 * A_HALO, d), BF16),
                        pltpu.VMEM((2, n_slab, tm + 2 * HALO, V7X_LANES), F32),
                        pltpu.VMEM((n_slab, tm, V7X_LANES), F32)],
        compiler_params=_params(1),
        name="inproj_a",
    )(x2, x2, x2, mod3, mod3, nw, w_xbc, cw, cb, w_dt, dt_b)


_REST_Z = 0
_REST_F = SSD_INNER
_REST_UV = _REST_F + FFT_WIDTH
_REST_GATE = _REST_UV + 2 * GMLP_WIDTH
_REST_COLS = _REST_GATE + N_BRANCH * D_MODEL


def _inproj_b_kernel(h_ref, w_ref, z_ref, f_ref, uv_ref, g_ref, *, tn):
    h = h_ref[...]
    for n in range(_REST_COLS // tn):
        c0 = n * tn
        r = _dot(h, w_ref[:, c0:c0 + tn])
        if c0 < _REST_F:
            z_ref[:, c0:c0 + tn] = _silu(r).astype(BF16)
        elif c0 < _REST_UV:
            f_ref[:, c0 - _REST_F:c0 - _REST_F + tn] = r.astype(BF16)
        elif c0 < _REST_GATE:
            uv_ref[:, c0 - _REST_UV:c0 - _REST_UV + tn] = _gelu_tanh(r).astype(BF16)
        else:
            g_ref[:, c0 - _REST_GATE:c0 - _REST_GATE + tn] = _sigmoid(r).astype(BF16)


def _inproj_b(h2, w_rest, tm):
    nt, d = h2.shape
    tn = 512
    widths = (SSD_INNER, FFT_WIDTH, 2 * GMLP_WIDTH, N_BRANCH * D_MODEL)
    return pl.pallas_call(
        functools.partial(_inproj_b_kernel, tn=tn),
        out_shape=tuple(jax.ShapeDtypeStruct((nt, w), BF16) for w in widths),
        grid=(nt // tm,),
        in_specs=[pl.BlockSpec((tm, d), lambda i: (i, 0)), _resident((d, _REST_COLS))],
        out_specs=tuple(pl.BlockSpec((tm, w), lambda i: (i, 0)) for w in widths),
        compiler_params=_params(1),
        name="inproj_b",
    )(h2, w_rest)


def _split3_pack(v, lane):
    hi = v.astype(BF16).astype(F32)
    r1 = v - hi
    mid = r1.astype(BF16).astype(F32)
    lo = r1 - mid
    return jnp.where(lane < SSD_HEADS, hi, jnp.where(lane < 2 * SSD_HEADS, mid, lo)).astype(BF16)


def _split3_stack(v):
    hi = v.astype(BF16)
    r1 = v - hi.astype(F32)
    mid = r1.astype(BF16)
    lo = (r1 - mid.astype(F32)).astype(BF16)
    return jnp.concatenate([hi, mid, lo], axis=0)


def _ssd_kernel(xbc_ref, dt_ref, z_ref, alog_ref, dskip_ref, nw_ref, e_ref, if_ref, ib_ref,
                *refs, n_rows, with_y):
    if with_y:
        y_ref, ff_ref, fb_ref, yacc_ref, ct_ref, st_ref = refs
    else:
        ff_ref, fb_ref, st_ref = refs
        y_ref = yacc_ref = ct_ref = None
    q = SSD_CHUNK
    nc = n_rows // q
    gw = SSD_INNER // SSD_GROUPS
    hpg = SSD_HEADS // SSD_GROUPS

    ri = lax.broadcasted_iota(jnp.int32, (q, q), 0)
    ci = lax.broadcasted_iota(jnp.int32, (q, q), 1)
    lane = ci
    low_mask = ci <= ri
    up_mask = ci >= ri
    tri_low = jnp.where(low_mask, 1.0, 0.0).astype(BF16)
    tri_up = jnp.where(up_mask, 1.0, 0.0).astype(BF16)
    tri3 = (jnp.concatenate([tri_low] * 3, axis=1), jnp.concatenate([tri_up] * 3, axis=1))
    masks = (low_mask, up_mask)
    first_half = lane < SSD_HEAD_DIM
    lane8 = lax.broadcasted_iota(jnp.int32, (V7X_SUBLANES, q), 1)

    if with_y:
        yacc_ref[...] = jnp.zeros(yacc_ref.shape, F32)

    for d in range(2):
        st_ref[...] = (if_ref if d == 0 else ib_ref)[0]
        a_row = -jnp.exp(alog_ref[d:d + 1, :])
        mask = masks[d]

        def chunk(k, carry, d=d, a_row=a_row, mask=mask):
            c = k if d == 0 else nc - 1 - k
            rows = pl.ds(pl.multiple_of(c * q, q), q)
            dt_c = dt_ref[rows, d * q:(d + 1) * q]
            la = dt_c * a_row
            cum = _dot(tri3[d], _split3_stack(la))
            tot = cum[q - 1:q, :] if d == 0 else cum[0:1, :]
            w = dt_c * jnp.exp(tot - cum)
            w_x = _dot(_split3_pack(w, lane), e_ref[...])
            etot = jnp.broadcast_to(jnp.exp(tot), (V7X_SUBLANES, q))
            etot_x = _dot(_split3_pack(etot, lane8), e_ref[...])[0:1, :]
            xs = xbc_ref[rows, 0:SSD_INNER]
            wx = xs * w_x.astype(BF16)
            if with_y:
                ct_ref[...] = (cum - jnp.log(dt_c)).T
            for g in range(SSD_GROUPS):
                bm = xbc_ref[rows, SSD_INNER + g * SSD_STATE:SSD_INNER + (g + 1) * SSD_STATE]
                cm = xbc_ref[rows, SSD_INNER + (SSD_GROUPS + g) * SSD_STATE:
                             SSD_INNER + (SSD_GROUPS + g + 1) * SSD_STATE]
                st = st_ref[g]
                if with_y:
                    cb = lax.dot_general(cm, bm, (((1,), (1,)), ((), ())),
                                         preferred_element_type=F32).astype(BF16)
                    st_b = st.astype(BF16)
                    for j in range(hpg // 2):
                        pair = g * (hpg // 2) + j
                        rhs = jnp.concatenate(
                            [xs[:, pair * q:(pair + 1) * q], st_b[:, j * q:(j + 1) * q]], axis=0)
                        ys = []
                        for t in range(2):
                            h = 2 * pair + t
                            colb = jnp.broadcast_to(cum[:, h:h + 1], (q, q))
                            arg = jnp.where(mask, colb - ct_ref[h:h + 1, :], NEG_BIG)
                            m_h = cb * jnp.exp(arg.astype(BF16))
                            cm_s = cm * jnp.exp(colb.astype(BF16))
                            lhs = jnp.concatenate([m_h, cm_s], axis=1)
                            ys.append(_dot(lhs, rhs))
                        y_pair = jnp.where(first_half, ys[0], ys[1])
                        yacc_ref[rows, pair * q:(pair + 1) * q] += y_pair
                s_new = lax.dot_general(bm, wx[:, g * gw:(g + 1) * gw], (((0,), (0,)), ((), ())),
                                        preferred_element_type=F32)
                st_ref[g] = st * etot_x[:, g * gw:(g + 1) * gw] + s_new
            return carry

        lax.fori_loop(0, nc, chunk, 0)
        (ff_ref if d == 0 else fb_ref)[0] = st_ref[...]

    if with_y:
        dsk = dskip_ref[...]
        nw = nw_ref[...]

        def fin(c, carry):
            rows = pl.ds(pl.multiple_of(c * q, q), q)
            xs = xbc_ref[rows, 0:SSD_INNER].astype(F32)
            y = (yacc_ref[rows, :] + dsk * xs) * z_ref[rows, :].astype(F32)
            for g in range(SSD_GROUPS):
                yg = y[:, g * gw:(g + 1) * gw]
                ms = jnp.mean(yg * yg, axis=-1, keepdims=True)
                y_ref[rows, g * gw:(g + 1) * gw] = (
                    (yg * lax.rsqrt(ms + EPS)) * nw[:, g * gw:(g + 1) * gw]).astype(BF16)
            return carry
        lax.fori_loop(0, nc, fin, 0)


def _ssd(xbc, dt, zs, alog, dskip, nw, e3, init_f, init_b, seq, with_y):
    nt = xbc.shape[0]
    nb = nt // seq
    ndt = dt.shape[1]
    st_shape = (SSD_GROUPS, SSD_STATE, SSD_INNER // SSD_GROUPS)
    st_spec = pl.BlockSpec((1,) + st_shape, lambda i: (i, 0, 0, 0))
    st_sds = jax.ShapeDtypeStruct((nb,) + st_shape, F32)
    in_specs = [pl.BlockSpec((seq, SSD_XBC), lambda i: (i, 0)),
                pl.BlockSpec((seq, ndt), lambda i: (i, 0)),
                pl.BlockSpec((seq, SSD_INNER), lambda i: (i, 0)),
                _resident(alog.shape), _resident((1, SSD_INNER)), _resident((1, SSD_INNER)),
                _resident(e3.shape), st_spec, st_spec]
    out_shape = [st_sds, st_sds]
    out_specs = [st_spec, st_spec]
    scratch = [pltpu.VMEM(st_shape, F32)]
    if with_y:
        out_shape = [jax.ShapeDtypeStruct((nt, SSD_INNER), BF16)] + out_shape
        out_specs = [pl.BlockSpec((seq, SSD_INNER), lambda i: (i, 0))] + out_specs
        scratch = [pltpu.VMEM((seq, SSD_INNER), F32),
                   pltpu.VMEM((SSD_CHUNK, V7X_LANES), F32)] + scratch
    return pl.pallas_call(
        functools.partial(_ssd_kernel, n_rows=seq, with_y=with_y),
        out_shape=tuple(out_shape),
        grid=(nb,),
        in_specs=in_specs,
        out_specs=tuple(out_specs),
        scratch_shapes=scratch,
        compiler_params=_params(1),
        name="ssd_y" if with_y else "ssd_state",
    )(xbc, dt, zs, alog, dskip, nw, e3, init_f, init_b)


def _fft_kernel(f_ref, cs_ref, dl_ref, o_ref, r_ref, *, n_rows):
    gd = FFT_GROUP_DIM
    mb = min(256, n_rows)

    def stage1(r, carry):
        rows = pl.ds(pl.multiple_of(r * mb, mb), mb)
        for g in range(FFT_GROUPS):
            t = _dot(f_ref[rows, g * gd:(g + 1) * gd], cs_ref[...])
            r_ref[rows, g * gd:(g + 1) * gd] = t[:, :gd].astype(BF16)
            r_ref[pl.ds(pl.multiple_of(n_rows + r * mb, mb), mb), g * gd:(g + 1) * gd] = t[:, gd:].astype(BF16)
        return carry
    lax.fori_loop(0, n_rows // mb, stage1, 0)

    def stage2(r, carry):
        rows = pl.ds(pl.multiple_of(r * mb, mb), mb)
        o_ref[rows, :] = _dot(dl_ref[rows, :], r_ref[...]).astype(BF16)
        return carry
    lax.fori_loop(0, n_rows // mb, stage2, 0)


def _dft_constants(seq):
    k = np.arange(seq, dtype=np.float64)
    ang = 2.0 * np.pi * np.outer(k, k) / seq
    dl = np.concatenate([np.cos(ang), -np.sin(ang)], axis=1) / math.sqrt(seq)
    c = np.arange(FFT_GROUP_DIM, dtype=np.float64)
    angc = 2.0 * np.pi * np.outer(c, c) / FFT_GROUP_DIM
    cs = np.concatenate([np.cos(angc), np.sin(angc)], axis=1) / math.sqrt(FFT_GROUP_DIM)
    return jnp.asarray(cs, BF16), jnp.asarray(dl, BF16)


def _fft(f2, seq):
    nt = f2.shape[0]
    cs, dl = _dft_constants(seq)
    return pl.pallas_call(
        functools.partial(_fft_kernel, n_rows=seq),
        out_shape=jax.ShapeDtypeStruct((nt, FFT_WIDTH), BF16),
        grid=(nt // seq,),
        in_specs=[pl.BlockSpec((seq, FFT_WIDTH), lambda i: (i, 0)),
                  _resident(cs.shape), _resident(dl.shape)],
        out_specs=pl.BlockSpec((seq, FFT_WIDTH), lambda i: (i, 0)),
        scratch_shapes=[pltpu.VMEM((2 * seq, FFT_WIDTH), BF16)],
        compiler_params=_params(1),
        name="fourier_mix",
    )(f2, cs, dl)


def _merge_kernel(ys_ref, yf_ref, uv_ref, g_ref, x_ref, gm_ref, w1_ref, w2_ref, w3_ref, wo_ref,
                  ws_ref, bs_ref, o_ref, yg_ref, m_ref, *, tm):
    gd = GMLP_GROUP_DIM
    for c in range(tm // GMLP_CHUNK):
        rows = slice(c * GMLP_CHUNK, (c + 1) * GMLP_CHUNK)
        for g in range(GMLP_GROUPS):
            v = uv_ref[rows, GMLP_WIDTH + g * gd:GMLP_WIDTH + (g + 1) * gd]
            s = _dot(ws_ref[g], v) + bs_ref[g]
            u = uv_ref[rows, g * gd:(g + 1) * gd].astype(F32)
            yg_ref[rows, g * gd:(g + 1) * gd] = (u * s).astype(BF16)
    tn = 512
    for n in range(D_MODEL // tn):
        cols = slice(n * tn, (n + 1) * tn)
        m = g_ref[:, n * tn:(n + 1) * tn].astype(F32) * _dot(ys_ref[...], w1_ref[:, cols])
        m += g_ref[:, D_MODEL + n * tn:D_MODEL + (n + 1) * tn].astype(F32) * _dot(yf_ref[...], w2_ref[:, cols])
        m += g_ref[:, 2 * D_MODEL + n * tn:2 * D_MODEL + (n + 1) * tn].astype(F32) * _dot(yg_ref[...], w3_ref[:, cols])
        m_ref[:, cols] = m.astype(BF16)
    o_ref[...] = x_ref[...] + gm_ref[0] * _dot(m_ref[...], wo_ref[...])


def _merge(ys, yf, uv, gate, x2, mod3, mod_row, w1, w2, w3, wo, ws, bs, seq, tm):
    nt, d = x2.shape
    per_seq = seq // tm
    row = lambda i: (i, 0)
    return pl.pallas_call(
        functools.partial(_merge_kernel, tm=tm),
        out_shape=jax.ShapeDtypeStruct((nt, d), F32),
        grid=(nt // tm,),
        in_specs=[pl.BlockSpec((tm, SSD_INNER), row),
                  pl.BlockSpec((tm, FFT_WIDTH), row),
                  pl.BlockSpec((tm, 2 * GMLP_WIDTH), row),
                  pl.BlockSpec((tm, N_BRANCH * d), row),
                  pl.BlockSpec((tm, d), row),
                  pl.BlockSpec((1, 1, d), lambda i: (mod_row(i // per_seq), 0, 2)),
                  _resident(w1.shape), _resident(w2.shape), _resident(w3.shape), _resident(wo.shape),
                  _resident(ws.shape), _resident(bs.shape)],
        out_specs=pl.BlockSpec((tm, d), row),
        scratch_shapes=[pltpu.VMEM((tm, GMLP_WIDTH), BF16), pltpu.VMEM((tm, d), BF16)],
        compiler_params=_params(1),
        name="merge",
    )(ys, yf, uv, gate, x2, mod3, w1, w2, w3, wo, ws, bs)


def _segment_masks(rc, width, taps, seg, n_rows):
    if seg >= n_rows:
        return None
    assert rc % seg == 0 and seg & (seg - 1) == 0
    pos = lax.broadcasted_iota(jnp.int32, (rc, width), 0) & (seg - 1)
    half = taps // 2
    return {off: (pos + off >= 0) & (pos + off < seg) for off in range(-half, half + 1) if off != 0}


def _row_conv(acc_ref, r0, rc, taps, cw, cb, masks):
    half = taps // 2
    out = None
    for k in range(taps):
        off = k - half
        v = acc_ref[pl.ds(HALO + r0 + off, rc), :]
        if masks is not None and off != 0:
            v = jnp.where(masks[off], v, 0.0)
        term = v * cw[k:k + 1, :]
        out = term if out is None else out + term
    return out + cb


def _zero_halo2d(acc_ref, n_rows):
    z = jnp.zeros((HALO, acc_ref.shape[1]), F32)
    acc_ref[0:HALO, :] = z
    acc_ref[HALO + n_rows:2 * HALO + n_rows, :] = z


def _ffn_kernel(x_ref, sh_ref, sc_ref, gm_ref, nw_ref, wu_ref, cw_ref, cb_ref, wd_ref, fw_ref,
                o_ref, h_ref, acca_ref, accv_ref, act_ref, *, tm, tf, seg, final_norm):
    rc = min(256, tm)
    _norm_modulate(x_ref, h_ref, nw_ref[...], sh_ref[0], 1.0 + sc_ref[0], tm, rc)
    _zero_halo2d(acca_ref, tm)
    _zero_halo2d(accv_ref, tm)
    masks = _segment_masks(rc, tf, FFN_CONV, seg, tm)
    h = h_ref[...]
    for c in range(D_FF // tf):
        ca = slice(c * tf, (c + 1) * tf)
        cv = slice(D_FF + c * tf, D_FF + (c + 1) * tf)
        acca_ref[HALO:HALO + tm, :] = _dot(h, wu_ref[:, ca])
        accv_ref[HALO:HALO + tm, :] = _dot(h, wu_ref[:, cv])
        for r in range(tm // rc):
            a = _row_conv(acca_ref, r * rc, rc, FFN_CONV, cw_ref[:, ca], cb_ref[:, ca], masks)
            v = _row_conv(accv_ref, r * rc, rc, FFN_CONV, cw_ref[:, cv], cb_ref[:, cv], masks)
            act_ref[r * rc:(r + 1) * rc, ca] = (_silu(a) * v).astype(BF16)
    y = x_ref[...] + gm_ref[0] * _dot(act_ref[...], wd_ref[...])
    if final_norm:
        ms = jnp.mean(y * y, axis=-1, keepdims=True)
        y = (y * lax.rsqrt(ms + EPS)) * fw_ref[...]
    o_ref[...] = y


def _ffn(x2, mod3, mod_row, nw, wu, cw, cb, wd, fw, seq, seg, tm, final_norm):
    nt, d = x2.shape
    per_seq = seq // tm
    tf = 256
    mrow = lambda k: (lambda i: (mod_row(i // per_seq), 0, k))
    return pl.pallas_call(
        functools.partial(_ffn_kernel, tm=tm, tf=tf, seg=seg, final_norm=final_norm),
        out_shape=jax.ShapeDtypeStruct((nt, d), F32),
        grid=(nt // tm,),
        in_specs=[pl.BlockSpec((tm, d), lambda i: (i, 0)),
                  pl.BlockSpec((1, 1, d), mrow(3)),
                  pl.BlockSpec((1, 1, d), mrow(4)),
                  pl.BlockSpec((1, 1, d), mrow(5)),
                  _resident((1, d)), _resident(wu.shape), _resident(cw.shape), _resident(cb.shape),
                  _resident(wd.shape), _resident((1, d))],
        out_specs=pl.BlockSpec((tm, d), lambda i: (i, 0)),
        scratch_shapes=[pltpu.VMEM((tm, d), BF16),
                        pltpu.VMEM((tm + 2 * HALO, tf), F32),
                        pltpu.VMEM((tm + 2 * HALO, tf), F32),
                        pltpu.VMEM((tm, D_FF), BF16)],
        compiler_params=_params(1),
        name="conv_ffn",
    )(x2, mod3, mod3, mod3, nw, wu, cw, cb, wd, fw)


def _rep_heads(v):
    pad = V7X_LANES - HEAD_REP * SSD_HEADS
    rep = jnp.concatenate([v] * HEAD_REP, axis=-1)
    return jnp.pad(rep, [(0, 0)] * (v.ndim - 1) + [(0, pad)])


def _head_expand_matrix():
    e = np.zeros((V7X_LANES, SSD_INNER), np.float32)
    for k in range(HEAD_REP * SSD_HEADS):
        h = k % SSD_HEADS
        e[k, h * SSD_HEAD_DIM:(h + 1) * SSD_HEAD_DIM] = 1.0
    return jnp.asarray(e, BF16)


def _pad_rows(a, rows):
    return jnp.pad(a, ((0, rows - a.shape[0]), (0, 0)))


def kernel(x, c, ctx, c_ctx, w_mod, b_mod, norm1_w, w_in, ssd_conv_w, ssd_conv_b, ssd_a_log,
           ssd_dt_bias, ssd_d, ssd_norm_w, gmlp_w_s, gmlp_b_s, w_ssd_o, w_fft_o, w_gmlp_o, w_out,
           norm2_w, ffn_w_up, ffn_conv_w, ffn_conv_b, ffn_w_down, final_norm_w):
    nb, seq, d = x.shape
    cseq = ctx.shape[1]
    depth = w_mod.shape[0]
    mod_rows = -(-(nb + 1) // V7X_SUBLANES) * V7X_SUBLANES
    cc = jnp.concatenate([c, c_ctx[None, :], jnp.zeros((mod_rows - nb - 1, d), F32)], axis=0)
    mod = _modulation(cc, w_mod, b_mod)
    lat_row = lambda i: i
    ctx_row = lambda i: nb
    e3 = _head_expand_matrix()
    tm_lat = 512
    tm_ctx = cseq

    xl = x.reshape(nb * seq, d)
    xc = ctx.reshape(nb * cseq, d)
    for i in range(depth):
        need_ctx = i < depth - 1
        mod3 = mod[i].reshape(mod_rows, 1, 6 * d)
        wi = w_in[i]
        w_xbc = wi[:, :SSD_XBC].astype(BF16)
        w_dt = jnp.concatenate([_rep_heads(wi[:, OFF_DT:OFF_DT + SSD_HEADS]),
                                _rep_heads(wi[:, OFF_DT + SSD_HEADS:OFF_Z])], axis=1).astype(BF16)
        dt_b = jnp.concatenate([_rep_heads(ssd_dt_bias[i, 0][None]), _rep_heads(ssd_dt_bias[i, 1][None])], axis=1)
        w_rest = wi[:, OFF_Z:].astype(BF16)
        cw = _pad_rows(ssd_conv_w[i], V7X_SUBLANES)
        cb = ssd_conv_b[i][None, :]
        alog = _pad_rows(_rep_heads(ssd_a_log[i]), V7X_SUBLANES)
        dskip = jnp.repeat(ssd_d[i], SSD_HEAD_DIM)[None, :]
        snw = ssd_norm_w[i][None, :]
        nw1 = norm1_w[i][None, :]
        nw2 = norm2_w[i][None, :]
        w1 = w_ssd_o[i].astype(BF16)
        w2 = w_fft_o[i].astype(BF16)
        w3 = w_gmlp_o[i].astype(BF16)
        wo = w_out[i].astype(BF16)
        ws = gmlp_w_s[i].astype(BF16)
        bs = jnp.broadcast_to(gmlp_b_s[i][:, :, None], (GMLP_GROUPS, GMLP_CHUNK, GMLP_GROUP_DIM))
        wu = ffn_w_up[i].astype(BF16)
        fcw = _pad_rows(ffn_conv_w[i], V7X_SUBLANES)
        fcb = ffn_conv_b[i][None, :]
        wd = ffn_w_down[i].astype(BF16)
        fw = final_norm_w[None, :]

        hc, xbc_c, dt_c = _inproj_a(xc, mod3, ctx_row, nw1, w_xbc, cw, cb, w_dt, dt_b, cseq)
        zero_state = jnp.zeros((nb, SSD_GROUPS, SSD_STATE, SSD_INNER // SSD_GROUPS), F32)
        if need_ctx:
            z_c, f_c, uv_c, g_c = _inproj_b(hc, w_rest, tm_ctx)
            y_c, s_f, s_b = _ssd(xbc_c, dt_c, z_c, alog, dskip, snw, e3, zero_state, zero_state, cseq, True)
        else:
            s_f, s_b = _ssd(xbc_c, dt_c, xbc_c[:, :SSD_INNER], alog, dskip, snw, e3, zero_state, zero_state,
                            cseq, False)

        hl, xbc_l, dt_l = _inproj_a(xl, mod3, lat_row, nw1, w_xbc, cw, cb, w_dt, dt_b, seq)
        z_l, f_l, uv_l, g_l = _inproj_b(hl, w_rest, 2 * tm_lat)
        y_l, _, _ = _ssd(xbc_l, dt_l, z_l, alog, dskip, snw, e3, s_f, s_b, seq, True)
        yf_l = _fft(f_l, seq)
        xl = _merge(y_l, yf_l, uv_l, g_l, xl, mod3, lat_row, w1, w2, w3, wo, ws, bs, seq, tm_lat)
        xl = _ffn(xl, mod3, lat_row, nw2, wu, fcw, fcb, wd, fw, seq, GRID_W, tm_lat,
                  final_norm=(i == depth - 1))
        if need_ctx:
            yf_c = _fft(f_c, cseq)
            xc = _merge(y_c, yf_c, uv_c, g_c, xc, mod3, ctx_row, w1, w2, w3, wo, ws, bs, cseq, tm_ctx)
            xc = _ffn(xc, mod3, ctx_row, nw2, wu, fcw, fcb, wd, fw, cseq, cseq, tm_ctx, final_norm=False)
    return xl.reshape(nb, seq, d)
```

```python
import functools
import math

import numpy as np
import jax
import jax.numpy as jnp
from jax import lax
from jax.experimental import pallas as pl
from jax.experimental.pallas import tpu as pltpu

F32 = jnp.float32
BF16 = jnp.bfloat16

D_MODEL = 1024
GRID_W = 64
SSD_HEADS = 16
SSD_HEAD_DIM = 64
SSD_INNER = SSD_HEADS * SSD_HEAD_DIM
SSD_GROUPS = 2
SSD_STATE = 128
SSD_CONV = 5
SSD_CHUNK = 128
SSD_XBC = SSD_INNER + 2 * SSD_GROUPS * SSD_STATE
SSD_COLS = SSD_XBC + 2 * SSD_HEADS
FFT_GROUPS = 4
FFT_GROUP_DIM = 128
FFT_WIDTH = FFT_GROUPS * FFT_GROUP_DIM
GMLP_GROUPS = 4
GMLP_GROUP_DIM = 128
GMLP_WIDTH = GMLP_GROUPS * GMLP_GROUP_DIM
GMLP_CHUNK = 128
N_BRANCH = 3
D_FF = 2816
FFN_CONV = 3
EPS = 1e-6

OFF_DT = SSD_XBC
OFF_Z = SSD_COLS
OFF_FFT = OFF_Z + SSD_INNER
OFF_GMLP = OFF_FFT + FFT_WIDTH
OFF_GATE = OFF_GMLP + 2 * GMLP_WIDTH

V7X_LANES = 128
V7X_SUBLANES = 8
V7X_VMEM_LIMIT_BYTES = 56 * 1024 * 1024

NEG_BIG = -1e30
HEAD_REP = 3
HALO = V7X_SUBLANES
CONV_STRIDE = 4
CONV_GROUP = CONV_STRIDE * V7X_SUBLANES


def _dot(a, b):
    return jnp.dot(a, b, preferred_element_type=F32)


def _sigmoid(x):
    return 1.0 / (1.0 + jnp.exp(-x))


def _silu(x):
    hx = 0.5 * x
    return hx + hx * jnp.tanh(hx)


def _gelu_tanh(x):
    return 0.5 * x * (1.0 + jnp.tanh(math.sqrt(2.0 / math.pi) * (x + 0.044715 * (x * x * x))))


def _softplus(x):
    return jnp.maximum(x, 0.0) + jnp.log(1.0 + jnp.exp(-jnp.abs(x)))


def _params(n_grid):
    return pltpu.CompilerParams(dimension_semantics=("arbitrary",) * n_grid,
                                vmem_limit_bytes=V7X_VMEM_LIMIT_BYTES)


def _resident(shape):
    nd = len(shape)
    return pl.BlockSpec(shape, lambda *_: (0,) * nd, pipeline_mode=pl.Buffered(1))


def _mod_kernel(c_ref, w_ref, b_ref, o_ref):
    s = _silu(c_ref[...]).astype(BF16)
    o_ref[0] = _dot(s, w_ref[0].astype(BF16)) + b_ref[0]


def _modulation(cc, w_mod, b_mod):
    depth, d, n = w_mod.shape
    rows = cc.shape[0]
    tn = 1536
    return pl.pallas_call(
        _mod_kernel,
        out_shape=jax.ShapeDtypeStruct((depth, rows, n), F32),
        grid=(depth, n // tn),
        in_specs=[pl.BlockSpec((rows, d), lambda l, j: (0, 0)),
                  pl.BlockSpec((1, d, tn), lambda l, j: (l, 0, j)),
                  pl.BlockSpec((1, 1, tn), lambda l, j: (l, 0, j))],
        out_specs=pl.BlockSpec((1, rows, tn), lambda l, j: (l, 0, j)),
        compiler_params=_params(2),
        name="modulation",
    )(cc, w_mod, b_mod.reshape(depth, 1, n))


def _norm_modulate(x_ref, h_ref, nw, sh, sc1, n_rows, rc):
    def body(r, carry):
        rows = pl.ds(pl.multiple_of(r * rc, rc), rc)
        xf = x_ref[rows, :]
        ms = jnp.mean(xf * xf, axis=-1, keepdims=True)
        hn = (xf * lax.rsqrt(ms + EPS)) * nw
        h_ref[rows, :] = (hn * sc1 + sh).astype(BF16)
        return carry
    lax.fori_loop(0, n_rows // rc, body, 0)


def _conv_group(acc_ref, lead, base, g, taps, w, b, seg, n_rows):
    half = taps // 2
    v = {j: acc_ref[lead + (pl.ds(base + j, V7X_SUBLANES, stride=CONV_STRIDE), slice(None))]
         for j in range(-half, CONV_STRIDE + half)}
    sub = lax.broadcasted_iota(jnp.int32, (V7X_SUBLANES, V7X_LANES), 0)
    outs = []
    for j in range(CONV_STRIDE):
        out = None
        for k in range(taps):
            off = k - half
            term = v[j + off]
            if seg < n_rows and off != 0:
                for m in range(V7X_SUBLANES):
                    pos = (g * CONV_GROUP + j + CONV_STRIDE * m) % seg
                    if not 0 <= pos + off < seg:
                        term = jnp.where(sub == m, 0.0, term)
            term = term * w[k:k + 1, :]
            out = term if out is None else out + term
        outs.append(out + b)
    return outs


def _zero_halo(acc_ref, n_rows):
    lead = acc_ref.shape[:-2]
    z = jnp.zeros(lead + (HALO, acc_ref.shape[-1]), F32)
    idx = (slice(None),) * len(lead)
    acc_ref[idx + (slice(0, HALO), slice(None))] = z
    acc_ref[idx + (slice(HALO + n_rows, 2 * HALO + n_rows), slice(None))] = z


A_HALO = 16


def _inproj_a_kernel(x_ref, xp_ref, xn_ref, sh_ref, sc_ref, nw_ref, w_ref, cw_ref, cb_ref, wdt_ref, dtb_ref,
                     h_ref, xbc_ref, dt_ref, hh_ref, acc_ref, tmp_ref, *, tm, n_blk, tn):
    nw = nw_ref[...]
    sh = sh_ref[0]
    sc1 = 1.0 + sc_ref[0]

    def norm_mod(xf):
        ms = jnp.mean(xf * xf, axis=-1, keepdims=True)
        return (((xf * lax.rsqrt(ms + EPS)) * nw) * sc1 + sh).astype(BF16)

    rc = min(256, tm)
    for r in range(tm // rc):
        h_ref[r * rc:(r + 1) * rc, :] = norm_mod(x_ref[r * rc:(r + 1) * rc, :])
    hh_ref[0:A_HALO, :] = norm_mod(xp_ref[...])
    hh_ref[A_HALO:2 * A_HALO, :] = norm_mod(xn_ref[...])
    h = h_ref[...]
    dt_ref[...] = _softplus(_dot(h, wdt_ref[...]) + dtb_ref[...])

    blk = pl.program_id(0) % n_blk
    n_slab = tn // V7X_LANES
    zero = jnp.zeros((HALO, V7X_LANES), F32)
    for n in range(SSD_XBC // tn + 1):
        if n < SSD_XBC // tn:
            wn = w_ref[:, n * tn:(n + 1) * tn]
            res = _dot(h, wn)
            halo = _dot(hh_ref[...], wn)
            for s in range(n_slab):
                lanes = slice(s * V7X_LANES, (s + 1) * V7X_LANES)
                acc_ref[n % 2, s, HALO:HALO + tm, :] = res[:, lanes]
                acc_ref[n % 2, s, 0:HALO, :] = jnp.where(blk == 0, zero, halo[A_HALO - HALO:A_HALO, lanes])
                acc_ref[n % 2, s, HALO + tm:2 * HALO + tm, :] = jnp.where(
                    blk == n_blk - 1, zero, halo[A_HALO:A_HALO + HALO, lanes])
        if n > 0:
            c0 = (n - 1) * tn
            for s in range(n_slab):
                cols = slice(c0 + s * V7X_LANES, c0 + (s + 1) * V7X_LANES)
                cw = cw_ref[:, cols]
                cb = cb_ref[:, cols]
                for g in range(tm // CONV_GROUP):
                    outs = _conv_group(acc_ref, ((n - 1) % 2, s), HALO + g * CONV_GROUP, g,
                                       SSD_CONV, cw, cb, tm, tm)
                    for j, u in enumerate(outs):
                        tmp_ref[s, pl.ds(g * CONV_GROUP + j, V7X_SUBLANES, stride=CONV_STRIDE), :] = _silu(u)
                xbc_ref[:, cols] = tmp_ref[s].astype(BF16)


def _inproj_a(x2, mod3, mod_row, nw, w_xbc, cw, cb, w_dt, dt_b, seq):
    nt, d = x2.shape
    tn = 512
    tm = min(512, seq)
    n_blk = seq // tm
    ndt = w_dt.shape[1]
    n_slab = tn // V7X_LANES
    per = tm // A_HALO
    last = nt // A_HALO - 1
    kern = functools.partial(_inproj_a_kernel, tm=tm, n_blk=n_blk, tn=tn)
    return pl.pallas_call(
        kern,
        out_shape=(jax.ShapeDtypeStruct((nt, d), BF16),
                   jax.ShapeDtypeStruct((nt, SSD_XBC), BF16),
                   jax.ShapeDtypeStruct((nt, ndt), F32)),
        grid=(nt // tm,),
        in_specs=[pl.BlockSpec((tm, d), lambda i: (i, 0)),
                  pl.BlockSpec((A_HALO, d), lambda i: (jnp.maximum(i * per - 1, 0), 0)),
                  pl.BlockSpec((A_HALO, d), lambda i: (jnp.minimum((i + 1) * per, last), 0)),
                  pl.BlockSpec((1, 1, d), lambda i: (mod_row(i // n_blk), 0, 0)),
                  pl.BlockSpec((1, 1, d), lambda i: (mod_row(i // n_blk), 0, 1)),
                  _resident((1, d)),
                  _resident((d, SSD_XBC)),
                  _resident(cw.shape),
                  _resident((1, SSD_XBC)),
                  _resident((d, ndt)),
                  _resident((1, ndt))],
        out_specs=(pl.BlockSpec((tm, d), lambda i: (i, 0)),
                   pl.BlockSpec((tm, SSD_XBC), lambda i: (i, 0)),
                   pl.BlockSpec((tm, ndt), lambda i: (i, 0))),
        scratch_shapes=[pltpu.VMEM((2 * A_HALO, d), BF16),
                        pltpu.VMEM((2, n_slab, tm + 2 * HALO, V7X_LANES), F32),
                        pltpu.VMEM((n_slab, tm, V7X_LANES), F32)],
        compiler_params=_params(1),
        name="inproj_a",
    )(x2, x2, x2, mod3, mod3, nw, w_xbc, cw, cb, w_dt, dt_b)


_REST_Z = 0
_REST_F = SSD_INNER
_REST_UV = _REST_F + FFT_WIDTH
_REST_GATE = _REST_UV + 2 * GMLP_WIDTH
_REST_COLS = _REST_GATE + N_BRANCH * D_MODEL


def _inproj_b_kernel(h_ref, w_ref, cs_ref, z_ref, fre_ref, fim_ref, uv_ref, g_ref, *, tn):
    h = h_ref[...]
    gd = FFT_GROUP_DIM
    for n in range(_REST_COLS // tn):
        c0 = n * tn
        r = _dot(h, w_ref[:, c0:c0 + tn])
        if c0 < _REST_F:
            z_ref[:, c0:c0 + tn] = _silu(r).astype(BF16)
        elif c0 < _REST_UV:
            f = r.astype(BF16)
            for g in range(tn // gd):
                t = _dot(f[:, g * gd:(g + 1) * gd], cs_ref[...])
                col = c0 - _REST_F + g * gd
                fre_ref[:, col:col + gd] = t[:, :gd].astype(BF16)
                fim_ref[:, col:col + gd] = t[:, gd:].astype(BF16)
        elif c0 < _REST_GATE:
            uv_ref[:, c0 - _REST_UV:c0 - _REST_UV + tn] = _gelu_tanh(r).astype(BF16)
        else:
            g_ref[:, c0 - _REST_GATE:c0 - _REST_GATE + tn] = _sigmoid(r).astype(BF16)


def _inproj_b(h2, w_rest, tm):
    nt, d = h2.shape
    tn = 512
    widths = (SSD_INNER, FFT_WIDTH, FFT_WIDTH, 2 * GMLP_WIDTH, N_BRANCH * D_MODEL)
    cs = _channel_dft_constants()
    return pl.pallas_call(
        functools.partial(_inproj_b_kernel, tn=tn),
        out_shape=tuple(jax.ShapeDtypeStruct((nt, w), BF16) for w in widths),
        grid=(nt // tm,),
        in_specs=[pl.BlockSpec((tm, d), lambda i: (i, 0)), _resident((d, _REST_COLS)), _resident(cs.shape)],
        out_specs=tuple(pl.BlockSpec((tm, w), lambda i: (i, 0)) for w in widths),
        compiler_params=_params(1),
        name="inproj_b",
    )(h2, w_rest, cs)


def _split3_pack(v, lane):
    hi = v.astype(BF16).astype(F32)
    r1 = v - hi
    mid = r1.astype(BF16).astype(F32)
    lo = r1 - mid
    return jnp.where(lane < SSD_HEADS, hi, jnp.where(lane < 2 * SSD_HEADS, mid, lo)).astype(BF16)


def _split3_stack(v):
    hi = v.astype(BF16)
    r1 = v - hi.astype(F32)
    mid = r1.astype(BF16)
    lo = (r1 - mid.astype(F32)).astype(BF16)
    return jnp.concatenate([hi, mid, lo], axis=0)


def _ssd_kernel(xbc_ref, dt_ref, z_ref, alog_ref, dskip_ref, nw_ref, e_ref, if_ref, ib_ref,
                *refs, n_rows, with_y):
    if with_y:
        y_ref, ff_ref, fb_ref, yacc_ref = refs[:4]
        slot_refs = refs[4:-1]
    else:
        ff_ref, fb_ref = refs[:2]
        y_ref = yacc_ref = None
        slot_refs = refs[2:-1]
    st_ref = refs[-1]
    per_slot = len(slot_refs) // 2
    slots = (slot_refs[:per_slot], slot_refs[per_slot:])
    q = SSD_CHUNK
    nc = n_rows // q
    gw = SSD_INNER // SSD_GROUPS
    hpg = SSD_HEADS // SSD_GROUPS

    ri = lax.broadcasted_iota(jnp.int32, (q, q), 0)
    ci = lax.broadcasted_iota(jnp.int32, (q, q), 1)
    lane = ci
    low_mask = ci <= ri
    up_mask = ci >= ri
    tri_low = jnp.where(low_mask, 1.0, 0.0).astype(BF16)
    tri_up = jnp.where(up_mask, 1.0, 0.0).astype(BF16)
    tri3 = (jnp.concatenate([tri_low] * 3, axis=1), jnp.concatenate([tri_up] * 3, axis=1))
    masks = (low_mask, up_mask)
    first_half = lane < SSD_HEAD_DIM
    lane8 = lax.broadcasted_iota(jnp.int32, (V7X_SUBLANES, q), 1)

    if with_y:
        yacc_ref[...] = jnp.zeros(yacc_ref.shape, F32)

    def prep(d, a_row, c, slot):
        if with_y:
            cum_ref, ct_ref, cb_ref, snew_ref, etot_ref = slots[slot]
        else:
            snew_ref, etot_ref = slots[slot]
        rows = pl.ds(pl.multiple_of(c * q, q), q)
        dt_c = dt_ref[rows, d * q:(d + 1) * q]
        la = dt_c * a_row
        cum = _dot(tri3[d], _split3_stack(la))
        tot = cum[q - 1:q, :] if d == 0 else cum[0:1, :]
        w = dt_c * jnp.exp(tot - cum)
        w_x = _dot(_split3_pack(w, lane), e_ref[...])
        etot = jnp.broadcast_to(jnp.exp(tot), (V7X_SUBLANES, q))
        etot_ref[...] = _dot(_split3_pack(etot, lane8), e_ref[...])
        wx = xbc_ref[rows, 0:SSD_INNER] * w_x.astype(BF16)
        if with_y:
            cum_ref[...] = cum
            ct_ref[...] = (cum - jnp.log(dt_c)).T
        for g in range(SSD_GROUPS):
            bm = xbc_ref[rows, SSD_INNER + g * SSD_STATE:SSD_INNER + (g + 1) * SSD_STATE]
            if with_y:
                cm = xbc_ref[rows, SSD_INNER + (SSD_GROUPS + g) * SSD_STATE:
                             SSD_INNER + (SSD_GROUPS + g + 1) * SSD_STATE]
                cb_ref[g] = lax.dot_general(cm, bm, (((1,), (1,)), ((), ())),
                                            preferred_element_type=F32).astype(BF16)
            snew_ref[g] = lax.dot_general(bm, wx[:, g * gw:(g + 1) * gw], (((0,), (0,)), ((), ())),
                                          preferred_element_type=F32)

    def heads(d, c, slot):
        if with_y:
            cum_ref, ct_ref, cb_ref, snew_ref, etot_ref = slots[slot]
        else:
            snew_ref, etot_ref = slots[slot]
        rows = pl.ds(pl.multiple_of(c * q, q), q)
        mask = masks[d]
        etot_x = etot_ref[0:1, :]
        if with_y:
            cum = cum_ref[...]
            xs = xbc_ref[rows, 0:SSD_INNER]
        for g in range(SSD_GROUPS):
            st = st_ref[g]
            if with_y:
                cm = xbc_ref[rows, SSD_INNER + (SSD_GROUPS + g) * SSD_STATE:
                             SSD_INNER + (SSD_GROUPS + g + 1) * SSD_STATE]
                cb = cb_ref[g]
                st_b = st.astype(BF16)
                for j in range(hpg // 2):
                    pair = g * (hpg // 2) + j
                    rhs = jnp.concatenate(
                        [xs[:, pair * q:(pair + 1) * q], st_b[:, j * q:(j + 1) * q]], axis=0)
                    ys = []
                    for t in range(2):
                        h = 2 * pair + t
                        colb = jnp.broadcast_to(cum[:, h:h + 1], (q, q))
                        arg = jnp.where(mask, colb - ct_ref[h:h + 1, :], NEG_BIG)
                        m_h = cb * jnp.exp(arg.astype(BF16))
                        cm_s = cm * jnp.exp(colb.astype(BF16))
                        lhs = jnp.concatenate([m_h, cm_s], axis=1)
                        ys.append(_dot(lhs, rhs))
                    y_pair = jnp.where(first_half, ys[0], ys[1])
                    yacc_ref[rows, pair * q:(pair + 1) * q] += y_pair
            st_ref[g] = st * etot_x[:, g * gw:(g + 1) * gw] + snew_ref[g]

    for d in range(2):
        st_ref[...] = (if_ref if d == 0 else ib_ref)[0]
        a_row = -jnp.exp(alog_ref[d:d + 1, :])
        chunk_of = (lambda k: k) if d == 0 else (lambda k: nc - 1 - k)

        prep(d, a_row, chunk_of(0), 0)

        def two_chunks(k2, carry, d=d, a_row=a_row, chunk_of=chunk_of):
            k = 2 * k2
            prep(d, a_row, chunk_of(k + 1), 1)
            heads(d, chunk_of(k), 0)
            prep(d, a_row, chunk_of(jnp.minimum(k + 2, nc - 1)), 0)
            heads(d, chunk_of(k + 1), 1)
            return carry

        lax.fori_loop(0, nc // 2, two_chunks, 0)
        (ff_ref if d == 0 else fb_ref)[0] = st_ref[...]

    if with_y:
        dsk = dskip_ref[...]
        nw = nw_ref[...]

        def fin(c, carry):
            rows = pl.ds(pl.multiple_of(c * q, q), q)
            xs = xbc_ref[rows, 0:SSD_INNER].astype(F32)
            y = (yacc_ref[rows, :] + dsk * xs) * z_ref[rows, :].astype(F32)
            for g in range(SSD_GROUPS):
                yg = y[:, g * gw:(g + 1) * gw]
                ms = jnp.mean(yg * yg, axis=-1, keepdims=True)
                y_ref[rows, g * gw:(g + 1) * gw] = (
                    (yg * lax.rsqrt(ms + EPS)) * nw[:, g * gw:(g + 1) * gw]).astype(BF16)
            return carry
        lax.fori_loop(0, nc, fin, 0)


def _ssd(xbc, dt, zs, alog, dskip, nw, e3, init_f, init_b, seq, with_y):
    nt = xbc.shape[0]
    nb = nt // seq
    ndt = dt.shape[1]
    assert (seq // SSD_CHUNK) % 2 == 0
    st_shape = (SSD_GROUPS, SSD_STATE, SSD_INNER // SSD_GROUPS)
    st_spec = pl.BlockSpec((1,) + st_shape, lambda i: (i, 0, 0, 0))
    st_sds = jax.ShapeDtypeStruct((nb,) + st_shape, F32)
    in_specs = [pl.BlockSpec((seq, SSD_XBC), lambda i: (i, 0)),
                pl.BlockSpec((seq, ndt), lambda i: (i, 0)),
                pl.BlockSpec((seq, SSD_INNER), lambda i: (i, 0)),
                _resident(alog.shape), _resident((1, SSD_INNER)), _resident((1, SSD_INNER)),
                _resident(e3.shape), st_spec, st_spec]
    out_shape = [st_sds, st_sds]
    out_specs = [st_spec, st_spec]
    slot = [pltpu.VMEM(st_shape, F32), pltpu.VMEM((V7X_SUBLANES, SSD_INNER), F32)]
    if with_y:
        slot = [pltpu.VMEM((SSD_CHUNK, V7X_LANES), F32), pltpu.VMEM((SSD_CHUNK, V7X_LANES), F32),
                pltpu.VMEM((SSD_GROUPS, SSD_CHUNK, SSD_STATE), BF16)] + slot
    scratch = slot + slot + [pltpu.VMEM(st_shape, F32)]
    if with_y:
        out_shape = [jax.ShapeDtypeStruct((nt, SSD_INNER), BF16)] + out_shape
        out_specs = [pl.BlockSpec((seq, SSD_INNER), lambda i: (i, 0))] + out_specs
        scratch = [pltpu.VMEM((seq, SSD_INNER), F32)] + scratch
    return pl.pallas_call(
        functools.partial(_ssd_kernel, n_rows=seq, with_y=with_y),
        out_shape=tuple(out_shape),
        grid=(nb,),
        in_specs=in_specs,
        out_specs=tuple(out_specs),
        scratch_shapes=scratch,
        compiler_params=_params(1),
        name="ssd_y" if with_y else "ssd_state",
    )(xbc, dt, zs, alog, dskip, nw, e3, init_f, init_b)


def _fft_split(seq):
    l1 = 1 << ((seq.bit_length() - 1) // 2)
    return l1, seq // l1


def _channel_dft_constants():
    c = np.arange(FFT_GROUP_DIM, dtype=np.float64)
    ang = 2.0 * np.pi * np.outer(c, c) / FFT_GROUP_DIM
    cs = np.concatenate([np.cos(ang), -np.sin(ang)], axis=1) / math.sqrt(FFT_GROUP_DIM)
    return jnp.asarray(cs, BF16)


def _position_dft_constants(seq):
    l1, l2 = _fft_split(seq)
    k1 = np.arange(l1, dtype=np.float64)
    ang1 = 2.0 * np.pi * np.outer(k1, k1) / l1
    c1, s1 = np.cos(ang1), np.sin(ang1)
    w1 = np.block([[c1, s1], [-s1, c1]]) / math.sqrt(seq)
    t2 = np.arange(l2, dtype=np.float64)
    k = k1[:, None, None] + l1 * np.arange(l2, dtype=np.float64)[None, :, None]
    phi = 2.0 * np.pi * ((k * t2[None, None, :]) % seq) / seq
    g = np.concatenate([np.cos(phi), np.sin(phi)], axis=2)
    return jnp.asarray(w1, BF16), jnp.asarray(g, BF16)


def _fft1_kernel(zre_ref, zim_ref, w1_ref, a_ref, *, l1, l2):
    cw = FFT_WIDTH
    per = min(l2, 8)
    for n in range(l2 // per):
        lanes = slice(n * per * cw, (n + 1) * per * cw)
        zz = jnp.concatenate([zre_ref[0, :, lanes], zim_ref[0, :, lanes]], axis=0)
        a = _dot(w1_ref[...], zz)
        for j in range(per):
            t2 = n * per + j
            for c in range(2):
                a_ref[0, c, t2] = a[c * l1:(c + 1) * l1, j * cw:(j + 1) * cw].astype(BF16)


def _fft2_kernel(a_ref, g_ref, o_ref, *, l1, l2):
    cw = FFT_WIDTH
    for k1 in range(l1):
        lanes = slice(k1 * cw, (k1 + 1) * cw)
        o_ref[0, :, lanes] = _dot(g_ref[k1], a_ref[0, :, lanes]).astype(BF16)


def _fft(zre, zim, seq):
    nt = zre.shape[0]
    nb = nt // seq
    cw = FFT_WIDTH
    l1, l2 = _fft_split(seq)
    w1, g = _position_dft_constants(seq)
    zspec = pl.BlockSpec((1, l1, l2 * cw), lambda i: (i, 0, 0))
    a = pl.pallas_call(
        functools.partial(_fft1_kernel, l1=l1, l2=l2),
        out_shape=jax.ShapeDtypeStruct((nb, 2, l2, l1, cw), BF16),
        grid=(nb,),
        in_specs=[zspec, zspec, _resident(w1.shape)],
        out_specs=pl.BlockSpec((1, 2, l2, l1, cw), lambda i: (i, 0, 0, 0, 0)),
        compiler_params=_params(1),
        name="fourier_step1",
    )(zre.reshape(nb, l1, l2 * cw), zim.reshape(nb, l1, l2 * cw), w1)
    y = pl.pallas_call(
        functools.partial(_fft2_kernel, l1=l1, l2=l2),
        out_shape=jax.ShapeDtypeStruct((nb, l2, l1 * cw), BF16),
        grid=(nb,),
        in_specs=[pl.BlockSpec((1, 2 * l2, l1 * cw), lambda i: (i, 0, 0)), _resident(g.shape)],
        out_specs=pl.BlockSpec((1, l2, l1 * cw), lambda i: (i, 0, 0)),
        compiler_params=_params(1),
        name="fourier_step2",
    )(a.reshape(nb, 2 * l2, l1 * cw), g)
    return y.reshape(nt, cw)


def _merge_kernel(ys_ref, yf_ref, uv_ref, g_ref, x_ref, gm_ref, w1_ref, w2_ref, w3_ref, wo_ref,
                  ws_ref, bs_ref, o_ref, yg_ref, m_ref, *, tm):
    gd = GMLP_GROUP_DIM
    for c in range(tm // GMLP_CHUNK):
        rows = slice(c * GMLP_CHUNK, (c + 1) * GMLP_CHUNK)
        for g in range(GMLP_GROUPS):
            v = uv_ref[rows, GMLP_WIDTH + g * gd:GMLP_WIDTH + (g + 1) * gd]
            s = _dot(ws_ref[g], v) + bs_ref[g]
            u = uv_ref[rows, g * gd:(g + 1) * gd].astype(F32)
            yg_ref[rows, g * gd:(g + 1) * gd] = (u * s).astype(BF16)
    tn = 512
    for n in range(D_MODEL // tn):
        cols = slice(n * tn, (n + 1) * tn)
        m = g_ref[:, n * tn:(n + 1) * tn].astype(F32) * _dot(ys_ref[...], w1_ref[:, cols])
        m += g_ref[:, D_MODEL + n * tn:D_MODEL + (n + 1) * tn].astype(F32) * _dot(yf_ref[...], w2_ref[:, cols])
        m += g_ref[:, 2 * D_MODEL + n * tn:2 * D_MODEL + (n + 1) * tn].astype(F32) * _dot(yg_ref[...], w3_ref[:, cols])
        m_ref[:, cols] = m.astype(BF16)
    o_ref[...] = x_ref[...] + gm_ref[0] * _dot(m_ref[...], wo_ref[...])


def _merge(ys, yf, uv, gate, x2, mod3, mod_row, w1, w2, w3, wo, ws, bs, seq, tm):
    nt, d = x2.shape
    per_seq = seq // tm
    row = lambda i: (i, 0)
    return pl.pallas_call(
        functools.partial(_merge_kernel, tm=tm),
        out_shape=jax.ShapeDtypeStruct((nt, d), F32),
        grid=(nt // tm,),
        in_specs=[pl.BlockSpec((tm, SSD_INNER), row),
                  pl.BlockSpec((tm, FFT_WIDTH), row),
                  pl.BlockSpec((tm, 2 * GMLP_WIDTH), row),
                  pl.BlockSpec((tm, N_BRANCH * d), row),
                  pl.BlockSpec((tm, d), row),
                  pl.BlockSpec((1, 1, d), lambda i: (mod_row(i // per_seq), 0, 2)),
                  _resident(w1.shape), _resident(w2.shape), _resident(w3.shape), _resident(wo.shape),
                  _resident(ws.shape), _resident(bs.shape)],
        out_specs=pl.BlockSpec((tm, d), row),
        scratch_shapes=[pltpu.VMEM((tm, GMLP_WIDTH), BF16), pltpu.VMEM((tm, d), BF16)],
        compiler_params=_params(1),
        name="merge",
    )(ys, yf, uv, gate, x2, mod3, w1, w2, w3, wo, ws, bs)


def _segment_masks(rc, width, taps, seg, n_rows):
    if seg >= n_rows:
        return None
    assert rc % seg == 0 and seg & (seg - 1) == 0
    pos = lax.broadcasted_iota(jnp.int32, (rc, width), 0) & (seg - 1)
    half = taps // 2
    return {off: (pos + off >= 0) & (pos + off < seg) for off in range(-half, half + 1) if off != 0}


def _row_conv(acc_ref, r0, rc, taps, cw, cb, masks):
    half = taps // 2
    out = None
    for k in range(taps):
        off = k - half
        v = acc_ref[pl.ds(HALO + r0 + off, rc), :]
        if masks is not None and off != 0:
            v = jnp.where(masks[off], v, 0.0)
        term = v * cw[k:k + 1, :]
        out = term if out is None else out + term
    return out + cb


def _zero_halo2d(acc_ref, n_rows):
    z = jnp.zeros((HALO, acc_ref.shape[1]), F32)
    acc_ref[0:HALO, :] = z
    acc_ref[HALO + n_rows:2 * HALO + n_rows, :] = z


def _ffn_kernel(x_ref, sh_ref, sc_ref, gm_ref, nw_ref, wu_ref, cw_ref, cb_ref, wd_ref, fw_ref,
                o_ref, h_ref, acca_ref, accv_ref, act_ref, *, tm, tf, seg, final_norm):
    rc = min(256, tm)
    _norm_modulate(x_ref, h_ref, nw_ref[...], sh_ref[0], 1.0 + sc_ref[0], tm, rc)
    _zero_halo2d(acca_ref, tm)
    _zero_halo2d(accv_ref, tm)
    masks = _segment_masks(rc, tf, FFN_CONV, seg, tm)
    h = h_ref[...]
    for c in range(D_FF // tf):
        ca = slice(c * tf, (c + 1) * tf)
        cv = slice(D_FF + c * tf, D_FF + (c + 1) * tf)
        acca_ref[HALO:HALO + tm, :] = _dot(h, wu_ref[:, ca])
        accv_ref[HALO:HALO + tm, :] = _dot(h, wu_ref[:, cv])
        for r in range(tm // rc):
            a = _row_conv(acca_ref, r * rc, rc, FFN_CONV, cw_ref[:, ca], cb_ref[:, ca], masks)
            v = _row_conv(accv_ref, r * rc, rc, FFN_CONV, cw_ref[:, cv], cb_ref[:, cv], masks)
            act_ref[r * rc:(r + 1) * rc, ca] = (_silu(a) * v).astype(BF16)
    y = x_ref[...] + gm_ref[0] * _dot(act_ref[...], wd_ref[...])
    if final_norm:
        ms = jnp.mean(y * y, axis=-1, keepdims=True)
        y = (y * lax.rsqrt(ms + EPS)) * fw_ref[...]
    o_ref[...] = y


def _ffn(x2, mod3, mod_row, nw, wu, cw, cb, wd, fw, seq, seg, tm, final_norm):
    nt, d = x2.shape
    per_seq = seq // tm
    tf = 256
    mrow = lambda k: (lambda i: (mod_row(i // per_seq), 0, k))
    return pl.pallas_call(
        functools.partial(_ffn_kernel, tm=tm, tf=tf, seg=seg, final_norm=final_norm),
        out_shape=jax.ShapeDtypeStruct((nt, d), F32),
        grid=(nt // tm,),
        in_specs=[pl.BlockSpec((tm, d), lambda i: (i, 0)),
                  pl.BlockSpec((1, 1, d), mrow(3)),
                  pl.BlockSpec((1, 1, d), mrow(4)),
                  pl.BlockSpec((1, 1, d), mrow(5)),
                  _resident((1, d)), _resident(wu.shape), _resident(cw.shape), _resident(cb.shape),
                  _resident(wd.shape), _resident((1, d))],
        out_specs=pl.BlockSpec((tm, d), lambda i: (i, 0)),
        scratch_shapes=[pltpu.VMEM((tm, d), BF16),
                        pltpu.VMEM((tm + 2 * HALO, tf), F32),
                        pltpu.VMEM((tm + 2 * HALO, tf), F32),
                        pltpu.VMEM((tm, D_FF), BF16)],
        compiler_params=_params(1),
        name="conv_ffn",
    )(x2, mod3, mod3, mod3, nw, wu, cw, cb, wd, fw)


def _rep_heads(v):
    pad = V7X_LANES - HEAD_REP * SSD_HEADS
    rep = jnp.concatenate([v] * HEAD_REP, axis=-1)
    return jnp.pad(rep, [(0, 0)] * (v.ndim - 1) + [(0, pad)])


def _head_expand_matrix():
    e = np.zeros((V7X_LANES, SSD_INNER), np.float32)
    for k in range(HEAD_REP * SSD_HEADS):
        h = k % SSD_HEADS
        e[k, h * SSD_HEAD_DIM:(h + 1) * SSD_HEAD_DIM] = 1.0
    return jnp.asarray(e, BF16)


def _pad_rows(a, rows):
    return jnp.pad(a, ((0, rows - a.shape[0]), (0, 0)))


def kernel(x, c, ctx, c_ctx, w_mod, b_mod, norm1_w, w_in, ssd_conv_w, ssd_conv_b, ssd_a_log,
           ssd_dt_bias, ssd_d, ssd_norm_w, gmlp_w_s, gmlp_b_s, w_ssd_o, w_fft_o, w_gmlp_o, w_out,
           norm2_w, ffn_w_up, ffn_conv_w, ffn_conv_b, ffn_w_down, final_norm_w):
    nb, seq, d = x.shape
    cseq = ctx.shape[1]
    depth = w_mod.shape[0]
    mod_rows = -(-(nb + 1) // V7X_SUBLANES) * V7X_SUBLANES
    cc = jnp.concatenate([c, c_ctx[None, :], jnp.zeros((mod_rows - nb - 1, d), F32)], axis=0)
    mod = _modulation(cc, w_mod, b_mod)
    lat_row = lambda i: i
    ctx_row = lambda i: nb
    e3 = _head_expand_matrix()
    tm_lat = 512
    tm_ctx = cseq

    xl = x.reshape(nb * seq, d)
    xc = ctx.reshape(nb * cseq, d)
    for i in range(depth):
        need_ctx = i < depth - 1
        mod3 = mod[i].reshape(mod_rows, 1, 6 * d)
        wi = w_in[i]
        w_xbc = wi[:, :SSD_XBC].astype(BF16)
        w_dt = jnp.concatenate([_rep_heads(wi[:, OFF_DT:OFF_DT + SSD_HEADS]),
                                _rep_heads(wi[:, OFF_DT + SSD_HEADS:OFF_Z])], axis=1).astype(BF16)
        dt_b = jnp.concatenate([_rep_heads(ssd_dt_bias[i, 0][None]), _rep_heads(ssd_dt_bias[i, 1][None])], axis=1)
        w_rest = wi[:, OFF_Z:].astype(BF16)
        cw = _pad_rows(ssd_conv_w[i], V7X_SUBLANES)
        cb = ssd_conv_b[i][None, :]
        alog = _pad_rows(_rep_heads(ssd_a_log[i]), V7X_SUBLANES)
        dskip = jnp.repeat(ssd_d[i], SSD_HEAD_DIM)[None, :]
        snw = ssd_norm_w[i][None, :]
        nw1 = norm1_w[i][None, :]
        nw2 = norm2_w[i][None, :]
        w1 = w_ssd_o[i].astype(BF16)
        w2 = w_fft_o[i].astype(BF16)
        w3 = w_gmlp_o[i].astype(BF16)
        wo = w_out[i].astype(BF16)
        ws = gmlp_w_s[i].astype(BF16)
        bs = jnp.broadcast_to(gmlp_b_s[i][:, :, None], (GMLP_GROUPS, GMLP_CHUNK, GMLP_GROUP_DIM))
        wu = ffn_w_up[i].astype(BF16)
        fcw = _pad_rows(ffn_conv_w[i], V7X_SUBLANES)
        fcb = ffn_conv_b[i][None, :]
        wd = ffn_w_down[i].astype(BF16)
        fw = final_norm_w[None, :]

        hc, xbc_c, dt_c = _inproj_a(xc, mod3, ctx_row, nw1, w_xbc, cw, cb, w_dt, dt_b, cseq)
        zero_state = jnp.zeros((nb, SSD_GROUPS, SSD_STATE, SSD_INNER // SSD_GROUPS), F32)
        if need_ctx:
            z_c, fre_c, fim_c, uv_c, g_c = _inproj_b(hc, w_rest, tm_ctx)
            y_c, s_f, s_b = _ssd(xbc_c, dt_c, z_c, alog, dskip, snw, e3, zero_state, zero_state, cseq, True)
        else:
            s_f, s_b = _ssd(xbc_c, dt_c, xbc_c[:, :SSD_INNER], alog, dskip, snw, e3, zero_state, zero_state,
                            cseq, False)

        hl, xbc_l, dt_l = _inproj_a(xl, mod3, lat_row, nw1, w_xbc, cw, cb, w_dt, dt_b, seq)
        z_l, fre_l, fim_l, uv_l, g_l = _inproj_b(hl, w_rest, 2 * tm_lat)
        y_l, _, _ = _ssd(xbc_l, dt_l, z_l, alog, dskip, snw, e3, s_f, s_b, seq, True)
        yf_l = _fft(fre_l, fim_l, seq)
        xl = _merge(y_l, yf_l, uv_l, g_l, xl, mod3, lat_row, w1, w2, w3, wo, ws, bs, seq, 2 * tm_lat)
        xl = _ffn(xl, mod3, lat_row, nw2, wu, fcw, fcb, wd, fw, seq, GRID_W, tm_lat,
                  final_norm=(i == depth - 1))
        if need_ctx:
            yf_c = _fft(fre_c, fim_c, cseq)
            xc = _merge(y_c, yf_c, uv_c, g_c, xc, mod3, ctx_row, w1, w2, w3, wo, ws, bs, cseq, tm_ctx)
            xc = _ffn(xc, mod3, ctx_row, nw2, wu, fcw, fcb, wd, fw, cseq, cseq, tm_ctx, final_norm=False)
    return xl.reshape(nb, seq, d)
```

```python
import functools
import math

import numpy as np
import jax
import jax.numpy as jnp
from jax import lax
from jax.experimental import pallas as pl
from jax.experimental.pallas import tpu as pltpu

F32 = jnp.float32
BF16 = jnp.bfloat16

D_MODEL = 1024
GRID_W = 64
SSD_HEADS = 16
SSD_HEAD_DIM = 64
SSD_INNER = SSD_HEADS * SSD_HEAD_DIM
SSD_GROUPS = 2
SSD_STATE = 128
SSD_CONV = 5
SSD_CHUNK = 128
SSD_XBC = SSD_INNER + 2 * SSD_GROUPS * SSD_STATE
SSD_COLS = SSD_XBC + 2 * SSD_HEADS
FFT_GROUPS = 4
FFT_GROUP_DIM = 128
FFT_WIDTH = FFT_GROUPS * FFT_GROUP_DIM
GMLP_GROUPS = 4
GMLP_GROUP_DIM = 128
GMLP_WIDTH = GMLP_GROUPS * GMLP_GROUP_DIM
GMLP_CHUNK = 128
N_BRANCH = 3
D_FF = 2816
FFN_CONV = 3
EPS = 1e-6

OFF_DT = SSD_XBC
OFF_Z = SSD_COLS
OFF_FFT = OFF_Z + SSD_INNER
OFF_GMLP = OFF_FFT + FFT_WIDTH
OFF_GATE = OFF_GMLP + 2 * GMLP_WIDTH

V7X_LANES = 128
V7X_SUBLANES = 8
V7X_VMEM_LIMIT_BYTES = 56 * 1024 * 1024

NEG_BIG = -1e30
HEAD_REP = 3
HALO = V7X_SUBLANES
CONV_STRIDE = 4
CONV_GROUP = CONV_STRIDE * V7X_SUBLANES


def _dot(a, b):
    return jnp.dot(a, b, preferred_element_type=F32)


def _sigmoid(x):
    return 1.0 / (1.0 + jnp.exp(-x))


def _silu(x):
    hx = 0.5 * x
    return hx + hx * jnp.tanh(hx)


def _gelu_tanh(x):
    return 0.5 * x * (1.0 + jnp.tanh(math.sqrt(2.0 / math.pi) * (x + 0.044715 * (x * x * x))))


def _softplus(x):
    return jnp.maximum(x, 0.0) + jnp.log(1.0 + jnp.exp(-jnp.abs(x)))


def _params(n_grid):
    return pltpu.CompilerParams(dimension_semantics=("arbitrary",) * n_grid,
                                vmem_limit_bytes=V7X_VMEM_LIMIT_BYTES)


def _resident(shape):
    nd = len(shape)
    return pl.BlockSpec(shape, lambda *_: (0,) * nd, pipeline_mode=pl.Buffered(1))


def _mod_kernel(c_ref, w_ref, b_ref, o_ref):
    s = _silu(c_ref[...]).astype(BF16)
    o_ref[0] = _dot(s, w_ref[0].astype(BF16)) + b_ref[0]


def _modulation(cc, w_mod, b_mod):
    depth, d, n = w_mod.shape
    rows = cc.shape[0]
    tn = 1536
    return pl.pallas_call(
        _mod_kernel,
        out_shape=jax.ShapeDtypeStruct((depth, rows, n), F32),
        grid=(depth, n // tn),
        in_specs=[pl.BlockSpec((rows, d), lambda l, j: (0, 0)),
                  pl.BlockSpec((1, d, tn), lambda l, j: (l, 0, j)),
                  pl.BlockSpec((1, 1, tn), lambda l, j: (l, 0, j))],
        out_specs=pl.BlockSpec((1, rows, tn), lambda l, j: (l, 0, j)),
        compiler_params=_params(2),
        name="modulation",
    )(cc, w_mod, b_mod.reshape(depth, 1, n))


def _norm_modulate(x_ref, h_ref, nw, sh, sc1, n_rows, rc):
    def body(r, carry):
        rows = pl.ds(pl.multiple_of(r * rc, rc), rc)
        xf = x_ref[rows, :]
        ms = jnp.mean(xf * xf, axis=-1, keepdims=True)
        hn = (xf * lax.rsqrt(ms + EPS)) * nw
        h_ref[rows, :] = (hn * sc1 + sh).astype(BF16)
        return carry
    lax.fori_loop(0, n_rows // rc, body, 0)


def _conv_group(acc_ref, lead, base, g, taps, w, b, seg, n_rows):
    half = taps // 2
    v = {j: acc_ref[lead + (pl.ds(base + j, V7X_SUBLANES, stride=CONV_STRIDE), slice(None))]
         for j in range(-half, CONV_STRIDE + half)}
    sub = lax.broadcasted_iota(jnp.int32, (V7X_SUBLANES, V7X_LANES), 0)
    outs = []
    for j in range(CONV_STRIDE):
        out = None
        for k in range(taps):
            off = k - half
            term = v[j + off]
            if seg < n_rows and off != 0:
                for m in range(V7X_SUBLANES):
                    pos = (g * CONV_GROUP + j + CONV_STRIDE * m) % seg
                    if not 0 <= pos + off < seg:
                        term = jnp.where(sub == m, 0.0, term)
            term = term * w[k:k + 1, :]
            out = term if out is None else out + term
        outs.append(out + b)
    return outs


def _zero_halo(acc_ref, n_rows):
    lead = acc_ref.shape[:-2]
    z = jnp.zeros(lead + (HALO, acc_ref.shape[-1]), F32)
    idx = (slice(None),) * len(lead)
    acc_ref[idx + (slice(0, HALO), slice(None))] = z
    acc_ref[idx + (slice(HALO + n_rows, 2 * HALO + n_rows), slice(None))] = z


A_HALO = 16


def _inproj_a_kernel(x_ref, xp_ref, xn_ref, sh_ref, sc_ref, nw_ref, w_ref, cw_ref, cb_ref, wdt_ref, dtb_ref,
                     h_ref, xbc_ref, dt_ref, hh_ref, acc_ref, tmp_ref, *, tm, n_blk, tn):
    nw = nw_ref[...]
    sh = sh_ref[0]
    sc1 = 1.0 + sc_ref[0]

    def norm_mod(xf):
        ms = jnp.mean(xf * xf, axis=-1, keepdims=True)
        return (((xf * lax.rsqrt(ms + EPS)) * nw) * sc1 + sh).astype(BF16)

    rc = min(256, tm)
    for r in range(tm // rc):
        h_ref[r * rc:(r + 1) * rc, :] = norm_mod(x_ref[r * rc:(r + 1) * rc, :])
    hh_ref[0:A_HALO, :] = norm_mod(xp_ref[...])
    hh_ref[A_HALO:2 * A_HALO, :] = norm_mod(xn_ref[...])
    h = h_ref[...]
    dt_ref[...] = _softplus(_dot(h, wdt_ref[...]) + dtb_ref[...])

    blk = pl.program_id(0) % n_blk
    n_slab = tn // V7X_LANES
    zero = jnp.zeros((HALO, V7X_LANES), F32)
    for n in range(SSD_XBC // tn + 1):
        if n < SSD_XBC // tn:
            wn = w_ref[:, n * tn:(n + 1) * tn]
            res = _dot(h, wn)
            halo = _dot(hh_ref[...], wn)
            for s in range(n_slab):
                lanes = slice(s * V7X_LANES, (s + 1) * V7X_LANES)
                acc_ref[n % 2, s, HALO:HALO + tm, :] = res[:, lanes]
                acc_ref[n % 2, s, 0:HALO, :] = jnp.where(blk == 0, zero, halo[A_HALO - HALO:A_HALO, lanes])
                acc_ref[n % 2, s, HALO + tm:2 * HALO + tm, :] = jnp.where(
                    blk == n_blk - 1, zero, halo[A_HALO:A_HALO + HALO, lanes])
        if n > 0:
            c0 = (n - 1) * tn
            for s in range(n_slab):
                cols = slice(c0 + s * V7X_LANES, c0 + (s + 1) * V7X_LANES)
                cw = cw_ref[:, cols]
                cb = cb_ref[:, cols]
                for g in range(tm // CONV_GROUP):
                    outs = _conv_group(acc_ref, ((n - 1) % 2, s), HALO + g * CONV_GROUP, g,
                                       SSD_CONV, cw, cb, tm, tm)
                    for j, u in enumerate(outs):
                        tmp_ref[s, pl.ds(g * CONV_GROUP + j, V7X_SUBLANES, stride=CONV_STRIDE), :] = _silu(u)
                xbc_ref[:, cols] = tmp_ref[s].astype(BF16)


def _inproj_a(x2, mod3, mod_row, nw, w_xbc, cw, cb, w_dt, dt_b, seq):
    nt, d = x2.shape
    tn = 512
    tm = min(512, seq)
    n_blk = seq // tm
    ndt = w_dt.shape[1]
    n_slab = tn // V7X_LANES
    per = tm // A_HALO
    last = nt // A_HALO - 1
    kern = functools.partial(_inproj_a_kernel, tm=tm, n_blk=n_blk, tn=tn)
    return pl.pallas_call(
        kern,
        out_shape=(jax.ShapeDtypeStruct((nt, d), BF16),
                   jax.ShapeDtypeStruct((nt, SSD_XBC), BF16),
                   jax.ShapeDtypeStruct((nt, ndt), F32)),
        grid=(nt // tm,),
        in_specs=[pl.BlockSpec((tm, d), lambda i: (i, 0)),
                  pl.BlockSpec((A_HALO, d), lambda i: (jnp.maximum(i * per - 1, 0), 0)),
                  pl.BlockSpec((A_HALO, d), lambda i: (jnp.minimum((i + 1) * per, last), 0)),
                  pl.BlockSpec((1, 1, d), lambda i: (mod_row(i // n_blk), 0, 0)),
                  pl.BlockSpec((1, 1, d), lambda i: (mod_row(i // n_blk), 0, 1)),
                  _resident((1, d)),
                  _resident((d, SSD_XBC)),
                  _resident(cw.shape),
                  _resident((1, SSD_XBC)),
                  _resident((d, ndt)),
                  _resident((1, ndt))],
        out_specs=(pl.BlockSpec((tm, d), lambda i: (i, 0)),
                   pl.BlockSpec((tm, SSD_XBC), lambda i: (i, 0)),
                   pl.BlockSpec((tm, ndt), lambda i: (i, 0))),
        scratch_shapes=[pltpu.VMEM((2 * A_HALO, d), BF16),
                        pltpu.VMEM((2, n_slab, tm + 2 * HALO, V7X_LANES), F32),
                        pltpu.VMEM((n_slab, tm, V7X_LANES), F32)],
        compiler_params=_params(1),
        name="inproj_a",
    )(x2, x2, x2, mod3, mod3, nw, w_xbc, cw, cb, w_dt, dt_b)


_REST_Z = 0
_REST_F = SSD_INNER
_REST_UV = _REST_F + FFT_WIDTH
_REST_GATE = _REST_UV + 2 * GMLP_WIDTH
_REST_COLS = _REST_GATE + N_BRANCH * D_MODEL


def _inproj_b_kernel(h_ref, w_ref, z_ref, f_ref, uv_ref, g_ref, *, tn):
    h = h_ref[...]
    for n in range(_REST_COLS // tn):
        c0 = n * tn
        r = _dot(h, w_ref[:, c0:c0 + tn])
        if c0 < _REST_F:
            z_ref[:, c0:c0 + tn] = _silu(r).astype(BF16)
        elif c0 < _REST_UV:
            f_ref[:, c0 - _REST_F:c0 - _REST_F + tn] = r.astype(BF16)
        elif c0 < _REST_GATE:
            uv_ref[:, c0 - _REST_UV:c0 - _REST_UV + tn] = _gelu_tanh(r).astype(BF16)
        else:
            g_ref[:, c0 - _REST_GATE:c0 - _REST_GATE + tn] = _sigmoid(r).astype(BF16)


def _inproj_b(h2, w_rest, tm):
    nt, d = h2.shape
    tn = 512
    widths = (SSD_INNER, FFT_WIDTH, 2 * GMLP_WIDTH, N_BRANCH * D_MODEL)
    return pl.pallas_call(
        functools.partial(_inproj_b_kernel, tn=tn),
        out_shape=tuple(jax.ShapeDtypeStruct((nt, w), BF16) for w in widths),
        grid=(nt // tm,),
        in_specs=[pl.BlockSpec((tm, d), lambda i: (i, 0)), _resident((d, _REST_COLS))],
        out_specs=tuple(pl.BlockSpec((tm, w), lambda i: (i, 0)) for w in widths),
        compiler_params=_params(1),
        name="inproj_b",
    )(h2, w_rest)


def _split3_pack(v, lane):
    hi = v.astype(BF16).astype(F32)
    r1 = v - hi
    mid = r1.astype(BF16).astype(F32)
    lo = r1 - mid
    return jnp.where(lane < SSD_HEADS, hi, jnp.where(lane < 2 * SSD_HEADS, mid, lo)).astype(BF16)


def _split3_stack(v):
    hi = v.astype(BF16)
    r1 = v - hi.astype(F32)
    mid = r1.astype(BF16)
    lo = (r1 - mid.astype(F32)).astype(BF16)
    return jnp.concatenate([hi, mid, lo], axis=0)


def _ssd_kernel(xbc_ref, dt_ref, z_ref, alog_ref, dskip_ref, nw_ref, e_ref, if_ref, ib_ref,
                *refs, n_rows, with_y):
    if with_y:
        y_ref, ff_ref, fb_ref, yacc_ref = refs[:4]
        slot_refs = refs[4:-1]
    else:
        ff_ref, fb_ref = refs[:2]
        y_ref = yacc_ref = None
        slot_refs = refs[2:-1]
    st_ref = refs[-1]
    per_slot = len(slot_refs) // 2
    slots = (slot_refs[:per_slot], slot_refs[per_slot:])
    q = SSD_CHUNK
    nc = n_rows // q
    gw = SSD_INNER // SSD_GROUPS
    hpg = SSD_HEADS // SSD_GROUPS

    ri = lax.broadcasted_iota(jnp.int32, (q, q), 0)
    ci = lax.broadcasted_iota(jnp.int32, (q, q), 1)
    lane = ci
    low_mask = ci <= ri
    up_mask = ci >= ri
    tri_low = jnp.where(low_mask, 1.0, 0.0).astype(BF16)
    tri_up = jnp.where(up_mask, 1.0, 0.0).astype(BF16)
    tri3 = (jnp.concatenate([tri_low] * 3, axis=1), jnp.concatenate([tri_up] * 3, axis=1))
    masks = (low_mask, up_mask)
    first_half = lane < SSD_HEAD_DIM
    lane8 = lax.broadcasted_iota(jnp.int32, (V7X_SUBLANES, q), 1)

    if with_y:
        yacc_ref[...] = jnp.zeros(yacc_ref.shape, F32)

    def prep(d, a_row, c, slot):
        if with_y:
            cum_ref, ct_ref, cb_ref, snew_ref, etot_ref = slots[slot]
        else:
            snew_ref, etot_ref = slots[slot]
        rows = pl.ds(pl.multiple_of(c * q, q), q)
        dt_c = dt_ref[rows, d * q:(d + 1) * q]
        la = dt_c * a_row
        cum = _dot(tri3[d], _split3_stack(la))
        tot = cum[q - 1:q, :] if d == 0 else cum[0:1, :]
        w = dt_c * jnp.exp(tot - cum)
        w_x = _dot(_split3_pack(w, lane), e_ref[...])
        etot = jnp.broadcast_to(jnp.exp(tot), (V7X_SUBLANES, q))
        etot_ref[...] = _dot(_split3_pack(etot, lane8), e_ref[...])
        wx = xbc_ref[rows, 0:SSD_INNER] * w_x.astype(BF16)
        if with_y:
            cum_ref[...] = cum
            ct_ref[...] = (cum - jnp.log(dt_c)).T
        for g in range(SSD_GROUPS):
            bm = xbc_ref[rows, SSD_INNER + g * SSD_STATE:SSD_INNER + (g + 1) * SSD_STATE]
            if with_y:
                cm = xbc_ref[rows, SSD_INNER + (SSD_GROUPS + g) * SSD_STATE:
                             SSD_INNER + (SSD_GROUPS + g + 1) * SSD_STATE]
                cb_ref[g] = lax.dot_general(cm, bm, (((1,), (1,)), ((), ())),
                                            preferred_element_type=F32).astype(BF16)
            snew_ref[g] = lax.dot_general(bm, wx[:, g * gw:(g + 1) * gw], (((0,), (0,)), ((), ())),
                                          preferred_element_type=F32)

    def heads(d, c, slot):
        if with_y:
            cum_ref, ct_ref, cb_ref, snew_ref, etot_ref = slots[slot]
        else:
            snew_ref, etot_ref = slots[slot]
        rows = pl.ds(pl.multiple_of(c * q, q), q)
        mask = masks[d]
        etot_x = etot_ref[0:1, :]
        if with_y:
            cum = cum_ref[...]
            xs = xbc_ref[rows, 0:SSD_INNER]
        for g in range(SSD_GROUPS):
            st = st_ref[g]
            if with_y:
                cm = xbc_ref[rows, SSD_INNER + (SSD_GROUPS + g) * SSD_STATE:
                             SSD_INNER + (SSD_GROUPS + g + 1) * SSD_STATE]
                cb = cb_ref[g]
                st_b = st.astype(BF16)
                for j in range(hpg // 2):
                    pair = g * (hpg // 2) + j
                    rhs = jnp.concatenate(
                        [xs[:, pair * q:(pair + 1) * q], st_b[:, j * q:(j + 1) * q]], axis=0)
                    ys = []
                    for t in range(2):
                        h = 2 * pair + t
                        colb = jnp.broadcast_to(cum[:, h:h + 1], (q, q))
                        arg = jnp.where(mask, colb - ct_ref[h:h + 1, :], NEG_BIG)
                        m_h = cb * jnp.exp(arg.astype(BF16))
                        cm_s = cm * jnp.exp(colb.astype(BF16))
                        lhs = jnp.concatenate([m_h, cm_s], axis=1)
                        ys.append(_dot(lhs, rhs))
                    y_pair = jnp.where(first_half, ys[0], ys[1])
                    yacc_ref[rows, pair * q:(pair + 1) * q] += y_pair
            st_ref[g] = st * etot_x[:, g * gw:(g + 1) * gw] + snew_ref[g]

    for d in range(2):
        st_ref[...] = (if_ref if d == 0 else ib_ref)[0]
        a_row = -jnp.exp(alog_ref[d:d + 1, :])
        chunk_of = (lambda k: k) if d == 0 else (lambda k: nc - 1 - k)

        prep(d, a_row, chunk_of(0), 0)

        def two_chunks(k2, carry, d=d, a_row=a_row, chunk_of=chunk_of):
            k = 2 * k2
            prep(d, a_row, chunk_of(k + 1), 1)
            heads(d, chunk_of(k), 0)
            prep(d, a_row, chunk_of(jnp.minimum(k + 2, nc - 1)), 0)
            heads(d, chunk_of(k + 1), 1)
            return carry

        lax.fori_loop(0, nc // 2, two_chunks, 0)
        (ff_ref if d == 0 else fb_ref)[0] = st_ref[...]

    if with_y:
        dsk = dskip_ref[...]
        nw = nw_ref[...]

        def fin(c, carry):
            rows = pl.ds(pl.multiple_of(c * q, q), q)
            xs = xbc_ref[rows, 0:SSD_INNER].astype(F32)
            y = (yacc_ref[rows, :] + dsk * xs) * z_ref[rows, :].astype(F32)
            for g in range(SSD_GROUPS):
                yg = y[:, g * gw:(g + 1) * gw]
                ms = jnp.mean(yg * yg, axis=-1, keepdims=True)
                y_ref[rows, g * gw:(g + 1) * gw] = (
                    (yg * lax.rsqrt(ms + EPS)) * nw[:, g * gw:(g + 1) * gw]).astype(BF16)
            return carry
        lax.fori_loop(0, nc, fin, 0)


def _ssd(xbc, dt, zs, alog, dskip, nw, e3, init_f, init_b, seq, with_y):
    nt = xbc.shape[0]
    nb = nt // seq
    ndt = dt.shape[1]
    assert (seq // SSD_CHUNK) % 2 == 0
    st_shape = (SSD_GROUPS, SSD_STATE, SSD_INNER // SSD_GROUPS)
    st_spec = pl.BlockSpec((1,) + st_shape, lambda i: (i, 0, 0, 0))
    st_sds = jax.ShapeDtypeStruct((nb,) + st_shape, F32)
    in_specs = [pl.BlockSpec((seq, SSD_XBC), lambda i: (i, 0)),
                pl.BlockSpec((seq, ndt), lambda i: (i, 0)),
                pl.BlockSpec((seq, SSD_INNER), lambda i: (i, 0)),
                _resident(alog.shape), _resident((1, SSD_INNER)), _resident((1, SSD_INNER)),
                _resident(e3.shape), st_spec, st_spec]
    out_shape = [st_sds, st_sds]
    out_specs = [st_spec, st_spec]
    slot = [pltpu.VMEM(st_shape, F32), pltpu.VMEM((V7X_SUBLANES, SSD_INNER), F32)]
    if with_y:
        slot = [pltpu.VMEM((SSD_CHUNK, V7X_LANES), F32), pltpu.VMEM((SSD_CHUNK, V7X_LANES), F32),
                pltpu.VMEM((SSD_GROUPS, SSD_CHUNK, SSD_STATE), BF16)] + slot
    scratch = slot + slot + [pltpu.VMEM(st_shape, F32)]
    if with_y:
        out_shape = [jax.ShapeDtypeStruct((nt, SSD_INNER), BF16)] + out_shape
        out_specs = [pl.BlockSpec((seq, SSD_INNER), lambda i: (i, 0))] + out_specs
        scratch = [pltpu.VMEM((seq, SSD_INNER), F32)] + scratch
    return pl.pallas_call(
        functools.partial(_ssd_kernel, n_rows=seq, with_y=with_y),
        out_shape=tuple(out_shape),
        grid=(nb,),
        in_specs=in_specs,
        out_specs=tuple(out_specs),
        scratch_shapes=scratch,
        compiler_params=_params(1),
        name="ssd_y" if with_y else "ssd_state",
    )(xbc, dt, zs, alog, dskip, nw, e3, init_f, init_b)


def _fft_kernel(f_ref, cs_ref, dl_ref, o_ref, r_ref, *, n_rows):
    gd = FFT_GROUP_DIM
    mb = min(256, n_rows)

    def stage1(r, carry):
        rows = pl.ds(pl.multiple_of(r * mb, mb), mb)
        for g in range(FFT_GROUPS):
            t = _dot(f_ref[rows, g * gd:(g + 1) * gd], cs_ref[...])
            r_ref[rows, g * gd:(g + 1) * gd] = t[:, :gd].astype(BF16)
            r_ref[pl.ds(pl.multiple_of(n_rows + r * mb, mb), mb), g * gd:(g + 1) * gd] = t[:, gd:].astype(BF16)
        return carry
    lax.fori_loop(0, n_rows // mb, stage1, 0)

    def stage2(r, carry):
        rows = pl.ds(pl.multiple_of(r * mb, mb), mb)
        o_ref[rows, :] = _dot(dl_ref[rows, :], r_ref[...]).astype(BF16)
        return carry
    lax.fori_loop(0, n_rows // mb, stage2, 0)


def _dft_constants(seq):
    k = np.arange(seq, dtype=np.float64)
    ang = 2.0 * np.pi * np.outer(k, k) / seq
    dl = np.concatenate([np.cos(ang), -np.sin(ang)], axis=1) / math.sqrt(seq)
    c = np.arange(FFT_GROUP_DIM, dtype=np.float64)
    angc = 2.0 * np.pi * np.outer(c, c) / FFT_GROUP_DIM
    cs = np.concatenate([np.cos(angc), np.sin(angc)], axis=1) / math.sqrt(FFT_GROUP_DIM)
    return jnp.asarray(cs, BF16), jnp.asarray(dl, BF16)


def _fft(f2, seq):
    nt = f2.shape[0]
    cs, dl = _dft_constants(seq)
    return pl.pallas_call(
        functools.partial(_fft_kernel, n_rows=seq),
        out_shape=jax.ShapeDtypeStruct((nt, FFT_WIDTH), BF16),
        grid=(nt // seq,),
        in_specs=[pl.BlockSpec((seq, FFT_WIDTH), lambda i: (i, 0)),
                  _resident(cs.shape), _resident(dl.shape)],
        out_specs=pl.BlockSpec((seq, FFT_WIDTH), lambda i: (i, 0)),
        scratch_shapes=[pltpu.VMEM((2 * seq, FFT_WIDTH), BF16)],
        compiler_params=_params(1),
        name="fourier_mix",
    )(f2, cs, dl)


def _merge_kernel(ys_ref, yf_ref, uv_ref, g_ref, x_ref, gm_ref, w1_ref, w2_ref, w3_ref, wo_ref,
                  ws_ref, bs_ref, o_ref, yg_ref, m_ref, *, tm):
    gd = GMLP_GROUP_DIM
    for c in range(tm // GMLP_CHUNK):
        rows = slice(c * GMLP_CHUNK, (c + 1) * GMLP_CHUNK)
        for g in range(GMLP_GROUPS):
            v = uv_ref[rows, GMLP_WIDTH + g * gd:GMLP_WIDTH + (g + 1) * gd]
            s = _dot(ws_ref[g], v) + bs_ref[g]
            u = uv_ref[rows, g * gd:(g + 1) * gd].astype(F32)
            yg_ref[rows, g * gd:(g + 1) * gd] = (u * s).astype(BF16)
    tn = 512
    for n in range(D_MODEL // tn):
        cols = slice(n * tn, (n + 1) * tn)
        m = g_ref[:, n * tn:(n + 1) * tn].astype(F32) * _dot(ys_ref[...], w1_ref[:, cols])
        m += g_ref[:, D_MODEL + n * tn:D_MODEL + (n + 1) * tn].astype(F32) * _dot(yf_ref[...], w2_ref[:, cols])
        m += g_ref[:, 2 * D_MODEL + n * tn:2 * D_MODEL + (n + 1) * tn].astype(F32) * _dot(yg_ref[...], w3_ref[:, cols])
        m_ref[:, cols] = m.astype(BF16)
    o_ref[...] = x_ref[...] + gm_ref[0] * _dot(m_ref[...], wo_ref[...])


def _merge(ys, yf, uv, gate, x2, mod3, mod_row, w1, w2, w3, wo, ws, bs, seq, tm):
    nt, d = x2.shape
    per_seq = seq // tm
    row = lambda i: (i, 0)
    return pl.pallas_call(
        functools.partial(_merge_kernel, tm=tm),
        out_shape=jax.ShapeDtypeStruct((nt, d), F32),
        grid=(nt // tm,),
        in_specs=[pl.BlockSpec((tm, SSD_INNER), row),
                  pl.BlockSpec((tm, FFT_WIDTH), row),
                  pl.BlockSpec((tm, 2 * GMLP_WIDTH), row),
                  pl.BlockSpec((tm, N_BRANCH * d), row),
                  pl.BlockSpec((tm, d), row),
                  pl.BlockSpec((1, 1, d), lambda i: (mod_row(i // per_seq), 0, 2)),
                  _resident(w1.shape), _resident(w2.shape), _resident(w3.shape), _resident(wo.shape),
                  _resident(ws.shape), _resident(bs.shape)],
        out_specs=pl.BlockSpec((tm, d), row),
        scratch_shapes=[pltpu.VMEM((tm, GMLP_WIDTH), BF16), pltpu.VMEM((tm, d), BF16)],
        compiler_params=_params(1),
        name="merge",
    )(ys, yf, uv, gate, x2, mod3, w1, w2, w3, wo, ws, bs)


def _segment_masks(rc, width, taps, seg, n_rows):
    if seg >= n_rows:
        return None
    assert rc % seg == 0 and seg & (seg - 1) == 0
    pos = lax.broadcasted_iota(jnp.int32, (rc, width), 0) & (seg - 1)
    half = taps // 2
    return {off: (pos + off >= 0) & (pos + off < seg) for off in range(-half, half + 1) if off != 0}


def _row_conv(acc_ref, r0, rc, taps, cw, cb, masks):
    half = taps // 2
    out = None
    for k in range(taps):
        off = k - half
        v = acc_ref[pl.ds(HALO + r0 + off, rc), :]
        if masks is not None and off != 0:
            v = jnp.where(masks[off], v, 0.0)
        term = v * cw[k:k + 1, :]
        out = term if out is None else out + term
    return out + cb


def _zero_halo2d(acc_ref, n_rows):
    z = jnp.zeros((HALO, acc_ref.shape[1]), F32)
    acc_ref[0:HALO, :] = z
    acc_ref[HALO + n_rows:2 * HALO + n_rows, :] = z


def _ffn_kernel(x_ref, sh_ref, sc_ref, gm_ref, nw_ref, wu_ref, cw_ref, cb_ref, wd_ref, fw_ref,
                o_ref, h_ref, acca_ref, accv_ref, act_ref, *, tm, tf, seg, final_norm):
    rc = min(256, tm)
    _norm_modulate(x_ref, h_ref, nw_ref[...], sh_ref[0], 1.0 + sc_ref[0], tm, rc)
    _zero_halo2d(acca_ref, tm)
    _zero_halo2d(accv_ref, tm)
    masks = _segment_masks(rc, tf, FFN_CONV, seg, tm)
    h = h_ref[...]
    for c in range(D_FF // tf):
        ca = slice(c * tf, (c + 1) * tf)
        cv = slice(D_FF + c * tf, D_FF + (c + 1) * tf)
        acca_ref[HALO:HALO + tm, :] = _dot(h, wu_ref[:, ca])
        accv_ref[HALO:HALO + tm, :] = _dot(h, wu_ref[:, cv])
        for r in range(tm // rc):
            a = _row_conv(acca_ref, r * rc, rc, FFN_CONV, cw_ref[:, ca], cb_ref[:, ca], masks)
            v = _row_conv(accv_ref, r * rc, rc, FFN_CONV, cw_ref[:, cv], cb_ref[:, cv], masks)
            act_ref[r * rc:(r + 1) * rc, ca] = (_silu(a) * v).astype(BF16)
    y = x_ref[...] + gm_ref[0] * _dot(act_ref[...], wd_ref[...])
    if final_norm:
        ms = jnp.mean(y * y, axis=-1, keepdims=True)
        y = (y * lax.rsqrt(ms + EPS)) * fw_ref[...]
    o_ref[...] = y


def _ffn(x2, mod3, mod_row, nw, wu, cw, cb, wd, fw, seq, seg, tm, final_norm):
    nt, d = x2.shape
    per_seq = seq // tm
    tf = 256
    mrow = lambda k: (lambda i: (mod_row(i // per_seq), 0, k))
    return pl.pallas_call(
        functools.partial(_ffn_kernel, tm=tm, tf=tf, seg=seg, final_norm=final_norm),
        out_shape=jax.ShapeDtypeStruct((nt, d), F32),
        grid=(nt // tm,),
        in_specs=[pl.BlockSpec((tm, d), lambda i: (i, 0)),
                  pl.BlockSpec((1, 1, d), mrow(3)),
                  pl.BlockSpec((1, 1, d), mrow(4)),
                  pl.BlockSpec((1, 1, d), mrow(5)),
                  _resident((1, d)), _resident(wu.shape), _resident(cw.shape), _resident(cb.shape),
                  _resident(wd.shape), _resident((1, d))],
        out_specs=pl.BlockSpec((tm, d), lambda i: (i, 0)),
        scratch_shapes=[pltpu.VMEM((tm, d), BF16),
                        pltpu.VMEM((tm + 2 * HALO, tf), F32),
                        pltpu.VMEM((tm + 2 * HALO, tf), F32),
                        pltpu.VMEM((tm, D_FF), BF16)],
        compiler_params=_params(1),
        name="conv_ffn",
    )(x2, mod3, mod3, mod3, nw, wu, cw, cb, wd, fw)


def _rep_heads(v):
    pad = V7X_LANES - HEAD_REP * SSD_HEADS
    rep = jnp.concatenate([v] * HEAD_REP, axis=-1)
    return jnp.pad(rep, [(0, 0)] * (v.ndim - 1) + [(0, pad)])


def _head_expand_matrix():
    e = np.zeros((V7X_LANES, SSD_INNER), np.float32)
    for k in range(HEAD_REP * SSD_HEADS):
        h = k % SSD_HEADS
        e[k, h * SSD_HEAD_DIM:(h + 1) * SSD_HEAD_DIM] = 1.0
    return jnp.asarray(e, BF16)


def _pad_rows(a, rows):
    return jnp.pad(a, ((0, rows - a.shape[0]), (0, 0)))


def kernel(x, c, ctx, c_ctx, w_mod, b_mod, norm1_w, w_in, ssd_conv_w, ssd_conv_b, ssd_a_log,
           ssd_dt_bias, ssd_d, ssd_norm_w, gmlp_w_s, gmlp_b_s, w_ssd_o, w_fft_o, w_gmlp_o, w_out,
           norm2_w, ffn_w_up, ffn_conv_w, ffn_conv_b, ffn_w_down, final_norm_w):
    nb, seq, d = x.shape
    cseq = ctx.shape[1]
    depth = w_mod.shape[0]
    mod_rows = -(-(nb + 1) // V7X_SUBLANES) * V7X_SUBLANES
    cc = jnp.concatenate([c, c_ctx[None, :], jnp.zeros((mod_rows - nb - 1, d), F32)], axis=0)
    mod = _modulation(cc, w_mod, b_mod)
    lat_row = lambda i: i
    ctx_row = lambda i: nb
    e3 = _head_expand_matrix()
    tm_lat = 512
    tm_ctx = cseq

    xl = x.reshape(nb * seq, d)
    xc = ctx.reshape(nb * cseq, d)
    for i in range(depth):
        need_ctx = i < depth - 1
        mod3 = mod[i].reshape(mod_rows, 1, 6 * d)
        wi = w_in[i]
        w_xbc = wi[:, :SSD_XBC].astype(BF16)
        w_dt = jnp.concatenate([_rep_heads(wi[:, OFF_DT:OFF_DT + SSD_HEADS]),
                                _rep_heads(wi[:, OFF_DT + SSD_HEADS:OFF_Z])], axis=1).astype(BF16)
        dt_b = jnp.concatenate([_rep_heads(ssd_dt_bias[i, 0][None]), _rep_heads(ssd_dt_bias[i, 1][None])], axis=1)
        w_rest = wi[:, OFF_Z:].astype(BF16)
        cw = _pad_rows(ssd_conv_w[i], V7X_SUBLANES)
        cb = ssd_conv_b[i][None, :]
        alog = _pad_rows(_rep_heads(ssd_a_log[i]), V7X_SUBLANES)
        dskip = jnp.repeat(ssd_d[i], SSD_HEAD_DIM)[None, :]
        snw = ssd_norm_w[i][None, :]
        nw1 = norm1_w[i][None, :]
        nw2 = norm2_w[i][None, :]
        w1 = w_ssd_o[i].astype(BF16)
        w2 = w_fft_o[i].astype(BF16)
        w3 = w_gmlp_o[i].astype(BF16)
        wo = w_out[i].astype(BF16)
        ws = gmlp_w_s[i].astype(BF16)
        bs = jnp.broadcast_to(gmlp_b_s[i][:, :, None], (GMLP_GROUPS, GMLP_CHUNK, GMLP_GROUP_DIM))
        wu = ffn_w_up[i].astype(BF16)
        fcw = _pad_rows(ffn_conv_w[i], V7X_SUBLANES)
        fcb = ffn_conv_b[i][None, :]
        wd = ffn_w_down[i].astype(BF16)
        fw = final_norm_w[None, :]

        hc, xbc_c, dt_c = _inproj_a(xc, mod3, ctx_row, nw1, w_xbc, cw, cb, w_dt, dt_b, cseq)
        zero_state = jnp.zeros((nb, SSD_GROUPS, SSD_STATE, SSD_INNER // SSD_GROUPS), F32)
        if need_ctx:
            z_c, f_c, uv_c, g_c = _inproj_b(hc, w_rest, tm_ctx)
            y_c, s_f, s_b = _ssd(xbc_c, dt_c, z_c, alog, dskip, snw, e3, zero_state, zero_state, cseq, True)
        else:
            s_f, s_b = _ssd(xbc_c, dt_c, xbc_c[:, :SSD_INNER], alog, dskip, snw, e3, zero_state, zero_state,
                            cseq, False)

        hl, xbc_l, dt_l = _inproj_a(xl, mod3, lat_row, nw1, w_xbc, cw, cb, w_dt, dt_b, seq)
        z_l, f_l, uv_l, g_l = _inproj_b(hl, w_rest, 2 * tm_lat)
        y_l, _, _ = _ssd(xbc_l, dt_l, z_l, alog, dskip, snw, e3, s_f, s_b, seq, True)
        yf_l = _fft(f_l, seq)
        xl = _merge(y_l, yf_l, uv_l, g_l, xl, mod3, lat_row, w1, w2, w3, wo, ws, bs, seq, 2 * tm_lat)
        xl = _ffn(xl, mod3, lat_row, nw2, wu, fcw, fcb, wd, fw, seq, GRID_W, 2 * tm_lat,
                  final_norm=(i == depth - 1))
        if need_ctx:
            yf_c = _fft(f_c, cseq)
            xc = _merge(y_c, yf_c, uv_c, g_c, xc, mod3, ctx_row, w1, w2, w3, wo, ws, bs, cseq, tm_ctx)
            xc = _ffn(xc, mod3, ctx_row, nw2, wu, fcw, fcb, wd, fw, cseq, cseq, tm_ctx, final_norm=False)
    return xl.reshape(nb, seq, d)
```

```python
import functools
import math

import numpy as np
import jax
import jax.numpy as jnp
from jax import lax
from jax.experimental import pallas as pl
from jax.experimental.pallas import tpu as pltpu

F32 = jnp.float32
BF16 = jnp.bfloat16

D_MODEL = 1024
GRID_W = 64
SSD_HEADS = 16
SSD_HEAD_DIM = 64
SSD_INNER = SSD_HEADS * SSD_HEAD_DIM
SSD_GROUPS = 2
SSD_STATE = 128
SSD_CONV = 5
SSD_CHUNK = 128
SSD_XBC = SSD_INNER + 2 * SSD_GROUPS * SSD_STATE
SSD_COLS = SSD_XBC + 2 * SSD_HEADS
FFT_GROUPS = 4
FFT_GROUP_DIM = 128
FFT_WIDTH = FFT_GROUPS * FFT_GROUP_DIM
GMLP_GROUPS = 4
GMLP_GROUP_DIM = 128
GMLP_WIDTH = GMLP_GROUPS * GMLP_GROUP_DIM
GMLP_CHUNK = 128
N_BRANCH = 3
D_FF = 2816
FFN_CONV = 3
EPS = 1e-6

OFF_DT = SSD_XBC
OFF_Z = SSD_COLS
OFF_FFT = OFF_Z + SSD_INNER
OFF_GMLP = OFF_FFT + FFT_WIDTH
OFF_GATE = OFF_GMLP + 2 * GMLP_WIDTH

V7X_LANES = 128
V7X_SUBLANES = 8
V7X_VMEM_LIMIT_BYTES = 56 * 1024 * 1024

NEG_BIG = -1e30
HEAD_REP = 3
HALO = V7X_SUBLANES
CONV_STRIDE = 4
CONV_GROUP = CONV_STRIDE * V7X_SUBLANES


def _dot(a, b):
    return jnp.dot(a, b, preferred_element_type=F32)


def _sigmoid(x):
    return 1.0 / (1.0 + jnp.exp(-x))


def _silu(x):
    hx = 0.5 * x
    return hx + hx * jnp.tanh(hx)


def _gelu_tanh(x):
    return 0.5 * x * (1.0 + jnp.tanh(math.sqrt(2.0 / math.pi) * (x + 0.044715 * (x * x * x))))


def _softplus(x):
    return jnp.maximum(x, 0.0) + jnp.log(1.0 + jnp.exp(-jnp.abs(x)))


def _params(n_grid):
    return pltpu.CompilerParams(dimension_semantics=("arbitrary",) * n_grid,
                                vmem_limit_bytes=V7X_VMEM_LIMIT_BYTES)


def _resident(shape):
    nd = len(shape)
    return pl.BlockSpec(shape, lambda *_: (0,) * nd, pipeline_mode=pl.Buffered(1))


def _mod_kernel(c_ref, w_ref, b_ref, o_ref):
    s = _silu(c_ref[...]).astype(BF16)
    o_ref[0] = _dot(s, w_ref[0].astype(BF16)) + b_ref[0]


def _modulation(cc, w_mod, b_mod):
    depth, d, n = w_mod.shape
    rows = cc.shape[0]
    tn = 1536
    return pl.pallas_call(
        _mod_kernel,
        out_shape=jax.ShapeDtypeStruct((depth, rows, n), F32),
        grid=(depth, n // tn),
        in_specs=[pl.BlockSpec((rows, d), lambda l, j: (0, 0)),
                  pl.BlockSpec((1, d, tn), lambda l, j: (l, 0, j)),
                  pl.BlockSpec((1, 1, tn), lambda l, j: (l, 0, j))],
        out_specs=pl.BlockSpec((1, rows, tn), lambda l, j: (l, 0, j)),
        compiler_params=_params(2),
        name="modulation",
    )(cc, w_mod, b_mod.reshape(depth, 1, n))


def _norm_modulate(x_ref, h_ref, nw, sh, sc1, n_rows, rc):
    def body(r, carry):
        rows = pl.ds(pl.multiple_of(r * rc, rc), rc)
        xf = x_ref[rows, :]
        ms = jnp.mean(xf * xf, axis=-1, keepdims=True)
        hn = (xf * lax.rsqrt(ms + EPS)) * nw
        h_ref[rows, :] = (hn * sc1 + sh).astype(BF16)
        return carry
    lax.fori_loop(0, n_rows // rc, body, 0)


def _conv_group(acc_ref, lead, base, g, taps, w, b, seg, n_rows):
    half = taps // 2
    v = {j: acc_ref[lead + (pl.ds(base + j, V7X_SUBLANES, stride=CONV_STRIDE), slice(None))]
         for j in range(-half, CONV_STRIDE + half)}
    sub = lax.broadcasted_iota(jnp.int32, (V7X_SUBLANES, V7X_LANES), 0)
    outs = []
    for j in range(CONV_STRIDE):
        out = None
        for k in range(taps):
            off = k - half
            term = v[j + off]
            if seg < n_rows and off != 0:
                for m in range(V7X_SUBLANES):
                    pos = (g * CONV_GROUP + j + CONV_STRIDE * m) % seg
                    if not 0 <= pos + off < seg:
                        term = jnp.where(sub == m, 0.0, term)
            term = term * w[k:k + 1, :]
            out = term if out is None else out + term
        outs.append(out + b)
    return outs


def _zero_halo(acc_ref, n_rows):
    lead = acc_ref.shape[:-2]
    z = jnp.zeros(lead + (HALO, acc_ref.shape[-1]), F32)
    idx = (slice(None),) * len(lead)
    acc_ref[idx + (slice(0, HALO), slice(None))] = z
    acc_ref[idx + (slice(HALO + n_rows, 2 * HALO + n_rows), slice(None))] = z


A_HALO = 16


def _inproj_a_kernel(x_ref, xp_ref, xn_ref, sh_ref, sc_ref, nw_ref, w_ref, cw_ref, cb_ref, wdt_ref, dtb_ref,
                     h_ref, xbc_ref, dt_ref, hh_ref, acc_ref, tmp_ref, *, tm, n_blk, tn):
    nw = nw_ref[...]
    sh = sh_ref[0]
    sc1 = 1.0 + sc_ref[0]

    def norm_mod(xf):
        ms = jnp.mean(xf * xf, axis=-1, keepdims=True)
        return (((xf * lax.rsqrt(ms + EPS)) * nw) * sc1 + sh).astype(BF16)

    rc = min(256, tm)
    for r in range(tm // rc):
        h_ref[r * rc:(r + 1) * rc, :] = norm_mod(x_ref[r * rc:(r + 1) * rc, :])
    hh_ref[0:A_HALO, :] = norm_mod(xp_ref[...])
    hh_ref[A_HALO:2 * A_HALO, :] = norm_mod(xn_ref[...])
    h = h_ref[...]
    dt_ref[...] = _softplus(_dot(h, wdt_ref[...]) + dtb_ref[...])

    blk = pl.program_id(0) % n_blk
    n_slab = tn // V7X_LANES
    zero = jnp.zeros((HALO, V7X_LANES), F32)
    for n in range(SSD_XBC // tn + 1):
        if n < SSD_XBC // tn:
            wn = w_ref[:, n * tn:(n + 1) * tn]
            res = _dot(h, wn)
            halo = _dot(hh_ref[...], wn)
            for s in range(n_slab):
                lanes = slice(s * V7X_LANES, (s + 1) * V7X_LANES)
                acc_ref[n % 2, s, HALO:HALO + tm, :] = res[:, lanes]
                acc_ref[n % 2, s, 0:HALO, :] = jnp.where(blk == 0, zero, halo[A_HALO - HALO:A_HALO, lanes])
                acc_ref[n % 2, s, HALO + tm:2 * HALO + tm, :] = jnp.where(
                    blk == n_blk - 1, zero, halo[A_HALO:A_HALO + HALO, lanes])
        if n > 0:
            c0 = (n - 1) * tn
            for s in range(n_slab):
                cols = slice(c0 + s * V7X_LANES, c0 + (s + 1) * V7X_LANES)
                cw = cw_ref[:, cols]
                cb = cb_ref[:, cols]
                for g in range(tm // CONV_GROUP):
                    outs = _conv_group(acc_ref, ((n - 1) % 2, s), HALO + g * CONV_GROUP, g,
                                       SSD_CONV, cw, cb, tm, tm)
                    for j, u in enumerate(outs):
                        tmp_ref[s, pl.ds(g * CONV_GROUP + j, V7X_SUBLANES, stride=CONV_STRIDE), :] = _silu(u)
                xbc_ref[:, cols] = tmp_ref[s].astype(BF16)


def _inproj_a(x2, mod3, mod_row, nw, w_xbc, cw, cb, w_dt, dt_b, seq):
    nt, d = x2.shape
    tn = 512
    tm = min(1024, seq)
    n_blk = seq // tm
    ndt = w_dt.shape[1]
    n_slab = tn // V7X_LANES
    per = tm // A_HALO
    last = nt // A_HALO - 1
    kern = functools.partial(_inproj_a_kernel, tm=tm, n_blk=n_blk, tn=tn)
    return pl.pallas_call(
        kern,
        out_shape=(jax.ShapeDtypeStruct((nt, d), BF16),
                   jax.ShapeDtypeStruct((nt, SSD_XBC), BF16),
                   jax.ShapeDtypeStruct((nt, ndt), F32)),
        grid=(nt // tm,),
        in_specs=[pl.BlockSpec((tm, d), lambda i: (i, 0)),
                  pl.BlockSpec((A_HALO, d), lambda i: (jnp.maximum(i * per - 1, 0), 0)),
                  pl.BlockSpec((A_HALO, d), lambda i: (jnp.minimum((i + 1) * per, last), 0)),
                  pl.BlockSpec((1, 1, d), lambda i: (mod_row(i // n_blk), 0, 0)),
                  pl.BlockSpec((1, 1, d), lambda i: (mod_row(i // n_blk), 0, 1)),
                  _resident((1, d)),
                  _resident((d, SSD_XBC)),
                  _resident(cw.shape),
                  _resident((1, SSD_XBC)),
                  _resident((d, ndt)),
                  _resident((1, ndt))],
        out_specs=(pl.BlockSpec((tm, d), lambda i: (i, 0)),
                   pl.BlockSpec((tm, SSD_XBC), lambda i: (i, 0)),
                   pl.BlockSpec((tm, ndt), lambda i: (i, 0))),
        scratch_shapes=[pltpu.VMEM((2 * A_HALO, d), BF16),
                        pltpu.VMEM((2, n_slab, tm + 2 * HALO, V7X_LANES), F32),
                        pltpu.VMEM((n_slab, tm, V7X_LANES), F32)],
        compiler_params=_params(1),
        name="inproj_a",
    )(x2, x2, x2, mod3, mod3, nw, w_xbc, cw, cb, w_dt, dt_b)


_REST_Z = 0
_REST_F = SSD_INNER
_REST_UV = _REST_F + FFT_WIDTH
_REST_GATE = _REST_UV + 2 * GMLP_WIDTH
_REST_COLS = _REST_GATE + N_BRANCH * D_MODEL


def _inproj_b_kernel(h_ref, w_ref, z_ref, f_ref, uv_ref, g_ref, *, tn):
    h = h_ref[...]
    for n in range(_REST_COLS // tn):
        c0 = n * tn
        r = _dot(h, w_ref[:, c0:c0 + tn])
        if c0 < _REST_F:
            z_ref[:, c0:c0 + tn] = _silu(r).astype(BF16)
        elif c0 < _REST_UV:
            f_ref[:, c0 - _REST_F:c0 - _REST_F + tn] = r.astype(BF16)
        elif c0 < _REST_GATE:
            uv_ref[:, c0 - _REST_UV:c0 - _REST_UV + tn] = _gelu_tanh(r).astype(BF16)
        else:
            g_ref[:, c0 - _REST_GATE:c0 - _REST_GATE + tn] = _sigmoid(r).astype(BF16)


def _inproj_b(h2, w_rest, tm):
    nt, d = h2.shape
    tn = 512
    widths = (SSD_INNER, FFT_WIDTH, 2 * GMLP_WIDTH, N_BRANCH * D_MODEL)
    return pl.pallas_call(
        functools.partial(_inproj_b_kernel, tn=tn),
        out_shape=tuple(jax.ShapeDtypeStruct((nt, w), BF16) for w in widths),
        grid=(nt // tm,),
        in_specs=[pl.BlockSpec((tm, d), lambda i: (i, 0)), _resident((d, _REST_COLS))],
        out_specs=tuple(pl.BlockSpec((tm, w), lambda i: (i, 0)) for w in widths),
        compiler_params=_params(1),
        name="inproj_b",
    )(h2, w_rest)


def _split3_pack(v, lane):
    hi = v.astype(BF16).astype(F32)
    r1 = v - hi
    mid = r1.astype(BF16).astype(F32)
    lo = r1 - mid
    return jnp.where(lane < SSD_HEADS, hi, jnp.where(lane < 2 * SSD_HEADS, mid, lo)).astype(BF16)


def _split3_stack(v):
    hi = v.astype(BF16)
    r1 = v - hi.astype(F32)
    mid = r1.astype(BF16)
    lo = (r1 - mid.astype(F32)).astype(BF16)
    return jnp.concatenate([hi, mid, lo], axis=0)


def _ssd_kernel(xbc_ref, dt_ref, z_ref, alog_ref, dskip_ref, nw_ref, e_ref, if_ref, ib_ref,
                *refs, n_rows, with_y):
    if with_y:
        y_ref, ff_ref, fb_ref, yacc_ref = refs[:4]
        slot_refs = refs[4:-1]
    else:
        ff_ref, fb_ref = refs[:2]
        y_ref = yacc_ref = None
        slot_refs = refs[2:-1]
    st_ref = refs[-1]
    per_slot = len(slot_refs) // 2
    slots = (slot_refs[:per_slot], slot_refs[per_slot:])
    q = SSD_CHUNK
    nc = n_rows // q
    gw = SSD_INNER // SSD_GROUPS
    hpg = SSD_HEADS // SSD_GROUPS

    ri = lax.broadcasted_iota(jnp.int32, (q, q), 0)
    ci = lax.broadcasted_iota(jnp.int32, (q, q), 1)
    lane = ci
    low_mask = ci <= ri
    up_mask = ci >= ri
    tri_low = jnp.where(low_mask, 1.0, 0.0).astype(BF16)
    tri_up = jnp.where(up_mask, 1.0, 0.0).astype(BF16)
    tri3 = (jnp.concatenate([tri_low] * 3, axis=1), jnp.concatenate([tri_up] * 3, axis=1))
    masks = (low_mask, up_mask)
    first_half = lane < SSD_HEAD_DIM
    lane8 = lax.broadcasted_iota(jnp.int32, (V7X_SUBLANES, q), 1)

    if with_y:
        yacc_ref[...] = jnp.zeros(yacc_ref.shape, F32)

    def prep(d, a_row, c, slot):
        if with_y:
            cum_ref, ct_ref, cb_ref, snew_ref, etot_ref = slots[slot]
        else:
            snew_ref, etot_ref = slots[slot]
        rows = pl.ds(pl.multiple_of(c * q, q), q)
        dt_c = dt_ref[rows, d * q:(d + 1) * q]
        la = dt_c * a_row
        cum = _dot(tri3[d], _split3_stack(la))
        tot = cum[q - 1:q, :] if d == 0 else cum[0:1, :]
        w = dt_c * jnp.exp(tot - cum)
        w_x = _dot(_split3_pack(w, lane), e_ref[...])
        etot = jnp.broadcast_to(jnp.exp(tot), (V7X_SUBLANES, q))
        etot_ref[...] = _dot(_split3_pack(etot, lane8), e_ref[...])
        wx = xbc_ref[rows, 0:SSD_INNER] * w_x.astype(BF16)
        if with_y:
            cum_ref[...] = cum
            ct_ref[...] = (cum - jnp.log(dt_c)).T
        for g in range(SSD_GROUPS):
            bm = xbc_ref[rows, SSD_INNER + g * SSD_STATE:SSD_INNER + (g + 1) * SSD_STATE]
            if with_y:
                cm = xbc_ref[rows, SSD_INNER + (SSD_GROUPS + g) * SSD_STATE:
                             SSD_INNER + (SSD_GROUPS + g + 1) * SSD_STATE]
                cb_ref[g] = lax.dot_general(cm, bm, (((1,), (1,)), ((), ())),
                                            preferred_element_type=F32).astype(BF16)
            snew_ref[g] = lax.dot_general(bm, wx[:, g * gw:(g + 1) * gw], (((0,), (0,)), ((), ())),
                                          preferred_element_type=F32)

    def heads(d, c, slot):
        if with_y:
            cum_ref, ct_ref, cb_ref, snew_ref, etot_ref = slots[slot]
        else:
            snew_ref, etot_ref = slots[slot]
        rows = pl.ds(pl.multiple_of(c * q, q), q)
        mask = masks[d]
        etot_x = etot_ref[0:1, :]
        if with_y:
            cum = cum_ref[...]
            xs = xbc_ref[rows, 0:SSD_INNER]
        for g in range(SSD_GROUPS):
            st = st_ref[g]
            if with_y:
                cm = xbc_ref[rows, SSD_INNER + (SSD_GROUPS + g) * SSD_STATE:
                             SSD_INNER + (SSD_GROUPS + g + 1) * SSD_STATE]
                cb = cb_ref[g]
                st_b = st.astype(BF16)
                for j in range(hpg // 2):
                    pair = g * (hpg // 2) + j
                    rhs = jnp.concatenate(
                        [xs[:, pair * q:(pair + 1) * q], st_b[:, j * q:(j + 1) * q]], axis=0)
                    ys = []
                    for t in range(2):
                        h = 2 * pair + t
                        colb = jnp.broadcast_to(cum[:, h:h + 1], (q, q))
                        arg = jnp.where(mask, colb - ct_ref[h:h + 1, :], NEG_BIG)
                        m_h = cb * jnp.exp(arg.astype(BF16))
                        cm_s = cm * jnp.exp(colb.astype(BF16))
                        lhs = jnp.concatenate([m_h, cm_s], axis=1)
                        ys.append(_dot(lhs, rhs))
                    y_pair = jnp.where(first_half, ys[0], ys[1])
                    yacc_ref[rows, pair * q:(pair + 1) * q] += y_pair
            st_ref[g] = st * etot_x[:, g * gw:(g + 1) * gw] + snew_ref[g]

    for d in range(2):
        st_ref[...] = (if_ref if d == 0 else ib_ref)[0]
        a_row = -jnp.exp(alog_ref[d:d + 1, :])
        chunk_of = (lambda k: k) if d == 0 else (lambda k: nc - 1 - k)

        prep(d, a_row, chunk_of(0), 0)

        def two_chunks(k2, carry, d=d, a_row=a_row, chunk_of=chunk_of):
            k = 2 * k2
            prep(d, a_row, chunk_of(k + 1), 1)
            heads(d, chunk_of(k), 0)
            prep(d, a_row, chunk_of(jnp.minimum(k + 2, nc - 1)), 0)
            heads(d, chunk_of(k + 1), 1)
            return carry

        lax.fori_loop(0, nc // 2, two_chunks, 0)
        (ff_ref if d == 0 else fb_ref)[0] = st_ref[...]

    if with_y:
        dsk = dskip_ref[...]
        nw = nw_ref[...]

        def fin(c, carry):
            rows = pl.ds(pl.multiple_of(c * q, q), q)
            xs = xbc_ref[rows, 0:SSD_INNER].astype(F32)
            y = (yacc_ref[rows, :] + dsk * xs) * z_ref[rows, :].astype(F32)
            for g in range(SSD_GROUPS):
                yg = y[:, g * gw:(g + 1) * gw]
                ms = jnp.mean(yg * yg, axis=-1, keepdims=True)
                y_ref[rows, g * gw:(g + 1) * gw] = (
                    (yg * lax.rsqrt(ms + EPS)) * nw[:, g * gw:(g + 1) * gw]).astype(BF16)
            return carry
        lax.fori_loop(0, nc, fin, 0)


def _ssd(xbc, dt, zs, alog, dskip, nw, e3, init_f, init_b, seq, with_y):
    nt = xbc.shape[0]
    nb = nt // seq
    ndt = dt.shape[1]
    assert (seq // SSD_CHUNK) % 2 == 0
    st_shape = (SSD_GROUPS, SSD_STATE, SSD_INNER // SSD_GROUPS)
    st_spec = pl.BlockSpec((1,) + st_shape, lambda i: (i, 0, 0, 0))
    st_sds = jax.ShapeDtypeStruct((nb,) + st_shape, F32)
    in_specs = [pl.BlockSpec((seq, SSD_XBC), lambda i: (i, 0)),
                pl.BlockSpec((seq, ndt), lambda i: (i, 0)),
                pl.BlockSpec((seq, SSD_INNER), lambda i: (i, 0)),
                _resident(alog.shape), _resident((1, SSD_INNER)), _resident((1, SSD_INNER)),
                _resident(e3.shape), st_spec, st_spec]
    out_shape = [st_sds, st_sds]
    out_specs = [st_spec, st_spec]
    slot = [pltpu.VMEM(st_shape, F32), pltpu.VMEM((V7X_SUBLANES, SSD_INNER), F32)]
    if with_y:
        slot = [pltpu.VMEM((SSD_CHUNK, V7X_LANES), F32), pltpu.VMEM((SSD_CHUNK, V7X_LANES), F32),
                pltpu.VMEM((SSD_GROUPS, SSD_CHUNK, SSD_STATE), BF16)] + slot
    scratch = slot + slot + [pltpu.VMEM(st_shape, F32)]
    if with_y:
        out_shape = [jax.ShapeDtypeStruct((nt, SSD_INNER), BF16)] + out_shape
        out_specs = [pl.BlockSpec((seq, SSD_INNER), lambda i: (i, 0))] + out_specs
        scratch = [pltpu.VMEM((seq, SSD_INNER), F32)] + scratch
    return pl.pallas_call(
        functools.partial(_ssd_kernel, n_rows=seq, with_y=with_y),
        out_shape=tuple(out_shape),
        grid=(nb,),
        in_specs=in_specs,
        out_specs=tuple(out_specs),
        scratch_shapes=scratch,
        compiler_params=_params(1),
        name="ssd_y" if with_y else "ssd_state",
    )(xbc, dt, zs, alog, dskip, nw, e3, init_f, init_b)


def _fft_kernel(f_ref, cs_ref, dl_ref, o_ref, r_ref, *, n_rows):
    gd = FFT_GROUP_DIM
    mb = min(256, n_rows)

    def stage1(r, carry):
        rows = pl.ds(pl.multiple_of(r * mb, mb), mb)
        for g in range(FFT_GROUPS):
            t = _dot(f_ref[rows, g * gd:(g + 1) * gd], cs_ref[...])
            r_ref[rows, g * gd:(g + 1) * gd] = t[:, :gd].astype(BF16)
            r_ref[pl.ds(pl.multiple_of(n_rows + r * mb, mb), mb), g * gd:(g + 1) * gd] = t[:, gd:].astype(BF16)
        return carry
    lax.fori_loop(0, n_rows // mb, stage1, 0)

    def stage2(r, carry):
        rows = pl.ds(pl.multiple_of(r * mb, mb), mb)
        o_ref[rows, :] = _dot(dl_ref[rows, :], r_ref[...]).astype(BF16)
        return carry
    lax.fori_loop(0, n_rows // mb, stage2, 0)


def _dft_constants(seq):
    k = np.arange(seq, dtype=np.float64)
    ang = 2.0 * np.pi * np.outer(k, k) / seq
    dl = np.concatenate([np.cos(ang), -np.sin(ang)], axis=1) / math.sqrt(seq)
    c = np.arange(FFT_GROUP_DIM, dtype=np.float64)
    angc = 2.0 * np.pi * np.outer(c, c) / FFT_GROUP_DIM
    cs = np.concatenate([np.cos(angc), np.sin(angc)], axis=1) / math.sqrt(FFT_GROUP_DIM)
    return jnp.asarray(cs, BF16), jnp.asarray(dl, BF16)


def _fft(f2, seq):
    nt = f2.shape[0]
    cs, dl = _dft_constants(seq)
    return pl.pallas_call(
        functools.partial(_fft_kernel, n_rows=seq),
        out_shape=jax.ShapeDtypeStruct((nt, FFT_WIDTH), BF16),
        grid=(nt // seq,),
        in_specs=[pl.BlockSpec((seq, FFT_WIDTH), lambda i: (i, 0)),
                  _resident(cs.shape), _resident(dl.shape)],
        out_specs=pl.BlockSpec((seq, FFT_WIDTH), lambda i: (i, 0)),
        scratch_shapes=[pltpu.VMEM((2 * seq, FFT_WIDTH), BF16)],
        compiler_params=_params(1),
        name="fourier_mix",
    )(f2, cs, dl)


def _merge_kernel(ys_ref, yf_ref, uv_ref, g_ref, x_ref, gm_ref, w1_ref, w2_ref, w3_ref, wo_ref,
                  ws_ref, bs_ref, o_ref, yg_ref, m_ref, *, tm):
    gd = GMLP_GROUP_DIM
    for c in range(tm // GMLP_CHUNK):
        rows = slice(c * GMLP_CHUNK, (c + 1) * GMLP_CHUNK)
        for g in range(GMLP_GROUPS):
            v = uv_ref[rows, GMLP_WIDTH + g * gd:GMLP_WIDTH + (g + 1) * gd]
            s = _dot(ws_ref[g], v) + bs_ref[g]
            u = uv_ref[rows, g * gd:(g + 1) * gd].astype(F32)
            yg_ref[rows, g * gd:(g + 1) * gd] = (u * s).astype(BF16)
    tn = 512
    for n in range(D_MODEL // tn):
        cols = slice(n * tn, (n + 1) * tn)
        m = g_ref[:, n * tn:(n + 1) * tn].astype(F32) * _dot(ys_ref[...], w1_ref[:, cols])
        m += g_ref[:, D_MODEL + n * tn:D_MODEL + (n + 1) * tn].astype(F32) * _dot(yf_ref[...], w2_ref[:, cols])
        m += g_ref[:, 2 * D_MODEL + n * tn:2 * D_MODEL + (n + 1) * tn].astype(F32) * _dot(yg_ref[...], w3_ref[:, cols])
        m_ref[:, cols] = m.astype(BF16)
    o_ref[...] = x_ref[...] + gm_ref[0] * _dot(m_ref[...], wo_ref[...])


def _merge(ys, yf, uv, gate, x2, mod3, mod_row, w1, w2, w3, wo, ws, bs, seq, tm):
    nt, d = x2.shape
    per_seq = seq // tm
    row = lambda i: (i, 0)
    return pl.pallas_call(
        functools.partial(_merge_kernel, tm=tm),
        out_shape=jax.ShapeDtypeStruct((nt, d), F32),
        grid=(nt // tm,),
        in_specs=[pl.BlockSpec((tm, SSD_INNER), row),
                  pl.BlockSpec((tm, FFT_WIDTH), row),
                  pl.BlockSpec((tm, 2 * GMLP_WIDTH), row),
                  pl.BlockSpec((tm, N_BRANCH * d), row),
                  pl.BlockSpec((tm, d), row),
                  pl.BlockSpec((1, 1, d), lambda i: (mod_row(i // per_seq), 0, 2)),
                  _resident(w1.shape), _resident(w2.shape), _resident(w3.shape), _resident(wo.shape),
                  _resident(ws.shape), _resident(bs.shape)],
        out_specs=pl.BlockSpec((tm, d), row),
        scratch_shapes=[pltpu.VMEM((tm, GMLP_WIDTH), BF16), pltpu.VMEM((tm, d), BF16)],
        compiler_params=_params(1),
        name="merge",
    )(ys, yf, uv, gate, x2, mod3, w1, w2, w3, wo, ws, bs)


def _segment_masks(rc, width, taps, seg, n_rows):
    if seg >= n_rows:
        return None
    assert rc % seg == 0 and seg & (seg - 1) == 0
    pos = lax.broadcasted_iota(jnp.int32, (rc, width), 0) & (seg - 1)
    half = taps // 2
    return {off: (pos + off >= 0) & (pos + off < seg) for off in range(-half, half + 1) if off != 0}


def _row_conv(acc_ref, r0, rc, taps, cw, cb, masks):
    half = taps // 2
    out = None
    for k in range(taps):
        off = k - half
        v = acc_ref[pl.ds(HALO + r0 + off, rc), :]
        if masks is not None and off != 0:
            v = jnp.where(masks[off], v, 0.0)
        term = v * cw[k:k + 1, :]
        out = term if out is None else out + term
    return out + cb


def _zero_halo2d(acc_ref, n_rows):
    z = jnp.zeros((HALO, acc_ref.shape[1]), F32)
    acc_ref[0:HALO, :] = z
    acc_ref[HALO + n_rows:2 * HALO + n_rows, :] = z


def _ffn_kernel(x_ref, sh_ref, sc_ref, gm_ref, nw_ref, wu_ref, cw_ref, cb_ref, wd_ref, fw_ref,
                o_ref, h_ref, acca_ref, accv_ref, act_ref, *, tm, tf, seg, final_norm):
    rc = min(256, tm)
    _norm_modulate(x_ref, h_ref, nw_ref[...], sh_ref[0], 1.0 + sc_ref[0], tm, rc)
    _zero_halo2d(acca_ref, tm)
    _zero_halo2d(accv_ref, tm)
    masks = _segment_masks(rc, tf, FFN_CONV, seg, tm)
    h = h_ref[...]
    for c in range(D_FF // tf):
        ca = slice(c * tf, (c + 1) * tf)
        cv = slice(D_FF + c * tf, D_FF + (c + 1) * tf)
        acca_ref[HALO:HALO + tm, :] = _dot(h, wu_ref[:, ca])
        accv_ref[HALO:HALO + tm, :] = _dot(h, wu_ref[:, cv])
        for r in range(tm // rc):
            a = _row_conv(acca_ref, r * rc, rc, FFN_CONV, cw_ref[:, ca], cb_ref[:, ca], masks)
            v = _row_conv(accv_ref, r * rc, rc, FFN_CONV, cw_ref[:, cv], cb_ref[:, cv], masks)
            act_ref[r * rc:(r + 1) * rc, ca] = (_silu(a) * v).astype(BF16)
    y = x_ref[...] + gm_ref[0] * _dot(act_ref[...], wd_ref[...])
    if final_norm:
        ms = jnp.mean(y * y, axis=-1, keepdims=True)
        y = (y * lax.rsqrt(ms + EPS)) * fw_ref[...]
    o_ref[...] = y


def _ffn(x2, mod3, mod_row, nw, wu, cw, cb, wd, fw, seq, seg, tm, final_norm):
    nt, d = x2.shape
    per_seq = seq // tm
    tf = 256
    mrow = lambda k: (lambda i: (mod_row(i // per_seq), 0, k))
    return pl.pallas_call(
        functools.partial(_ffn_kernel, tm=tm, tf=tf, seg=seg, final_norm=final_norm),
        out_shape=jax.ShapeDtypeStruct((nt, d), F32),
        grid=(nt // tm,),
        in_specs=[pl.BlockSpec((tm, d), lambda i: (i, 0)),
                  pl.BlockSpec((1, 1, d), mrow(3)),
                  pl.BlockSpec((1, 1, d), mrow(4)),
                  pl.BlockSpec((1, 1, d), mrow(5)),
                  _resident((1, d)), _resident(wu.shape), _resident(cw.shape), _resident(cb.shape),
                  _resident(wd.shape), _resident((1, d))],
        out_specs=pl.BlockSpec((tm, d), lambda i: (i, 0)),
        scratch_shapes=[pltpu.VMEM((tm, d), BF16),
                        pltpu.VMEM((tm + 2 * HALO, tf), F32),
                        pltpu.VMEM((tm + 2 * HALO, tf), F32),
                        pltpu.VMEM((tm, D_FF), BF16)],
        compiler_params=_params(1),
        name="conv_ffn",
    )(x2, mod3, mod3, mod3, nw, wu, cw, cb, wd, fw)


def _rep_heads(v):
    pad = V7X_LANES - HEAD_REP * SSD_HEADS
    rep = jnp.concatenate([v] * HEAD_REP, axis=-1)
    return jnp.pad(rep, [(0, 0)] * (v.ndim - 1) + [(0, pad)])


def _head_expand_matrix():
    e = np.zeros((V7X_LANES, SSD_INNER), np.float32)
    for k in range(HEAD_REP * SSD_HEADS):
        h = k % SSD_HEADS
        e[k, h * SSD_HEAD_DIM:(h + 1) * SSD_HEAD_DIM] = 1.0
    return jnp.asarray(e, BF16)


def _pad_rows(a, rows):
    return jnp.pad(a, ((0, rows - a.shape[0]), (0, 0)))


def kernel(x, c, ctx, c_ctx, w_mod, b_mod, norm1_w, w_in, ssd_conv_w, ssd_conv_b, ssd_a_log,
           ssd_dt_bias, ssd_d, ssd_norm_w, gmlp_w_s, gmlp_b_s, w_ssd_o, w_fft_o, w_gmlp_o, w_out,
           norm2_w, ffn_w_up, ffn_conv_w, ffn_conv_b, ffn_w_down, final_norm_w):
    nb, seq, d = x.shape
    cseq = ctx.shape[1]
    depth = w_mod.shape[0]
    mod_rows = -(-(nb + 1) // V7X_SUBLANES) * V7X_SUBLANES
    cc = jnp.concatenate([c, c_ctx[None, :], jnp.zeros((mod_rows - nb - 1, d), F32)], axis=0)
    mod = _modulation(cc, w_mod, b_mod)
    lat_row = lambda i: i
    ctx_row = lambda i: nb
    e3 = _head_expand_matrix()
    tm_lat = 512
    tm_ctx = cseq

    xl = x.reshape(nb * seq, d)
    xc = ctx.reshape(nb * cseq, d)
    for i in range(depth):
        need_ctx = i < depth - 1
        mod3 = mod[i].reshape(mod_rows, 1, 6 * d)
        wi = w_in[i]
        w_xbc = wi[:, :SSD_XBC].astype(BF16)
        w_dt = jnp.concatenate([_rep_heads(wi[:, OFF_DT:OFF_DT + SSD_HEADS]),
                                _rep_heads(wi[:, OFF_DT + SSD_HEADS:OFF_Z])], axis=1).astype(BF16)
        dt_b = jnp.concatenate([_rep_heads(ssd_dt_bias[i, 0][None]), _rep_heads(ssd_dt_bias[i, 1][None])], axis=1)
        w_rest = wi[:, OFF_Z:].astype(BF16)
        cw = _pad_rows(ssd_conv_w[i], V7X_SUBLANES)
        cb = ssd_conv_b[i][None, :]
        alog = _pad_rows(_rep_heads(ssd_a_log[i]), V7X_SUBLANES)
        dskip = jnp.repeat(ssd_d[i], SSD_HEAD_DIM)[None, :]
        snw = ssd_norm_w[i][None, :]
        nw1 = norm1_w[i][None, :]
        nw2 = norm2_w[i][None, :]
        w1 = w_ssd_o[i].astype(BF16)
        w2 = w_fft_o[i].astype(BF16)
        w3 = w_gmlp_o[i].astype(BF16)
        wo = w_out[i].astype(BF16)
        ws = gmlp_w_s[i].astype(BF16)
        bs = jnp.broadcast_to(gmlp_b_s[i][:, :, None], (GMLP_GROUPS, GMLP_CHUNK, GMLP_GROUP_DIM))
        wu = ffn_w_up[i].astype(BF16)
        fcw = _pad_rows(ffn_conv_w[i], V7X_SUBLANES)
        fcb = ffn_conv_b[i][None, :]
        wd = ffn_w_down[i].astype(BF16)
        fw = final_norm_w[None, :]

        hc, xbc_c, dt_c = _inproj_a(xc, mod3, ctx_row, nw1, w_xbc, cw, cb, w_dt, dt_b, cseq)
        zero_state = jnp.zeros((nb, SSD_GROUPS, SSD_STATE, SSD_INNER // SSD_GROUPS), F32)
        if need_ctx:
            z_c, f_c, uv_c, g_c = _inproj_b(hc, w_rest, tm_ctx)
            y_c, s_f, s_b = _ssd(xbc_c, dt_c, z_c, alog, dskip, snw, e3, zero_state, zero_state, cseq, True)
        else:
            s_f, s_b = _ssd(xbc_c, dt_c, xbc_c[:, :SSD_INNER], alog, dskip, snw, e3, zero_state, zero_state,
                            cseq, False)

        hl, xbc_l, dt_l = _inproj_a(xl, mod3, lat_row, nw1, w_xbc, cw, cb, w_dt, dt_b, seq)
        z_l, f_l, uv_l, g_l = _inproj_b(hl, w_rest, 2 * tm_lat)
        y_l, _, _ = _ssd(xbc_l, dt_l, z_l, alog, dskip, snw, e3, s_f, s_b, seq, True)
        yf_l = _fft(f_l, seq)
        xl = _merge(y_l, yf_l, uv_l, g_l, xl, mod3, lat_row, w1, w2, w3, wo, ws, bs, seq, 2 * tm_lat)
        xl = _ffn(xl, mod3, lat_row, nw2, wu, fcw, fcb, wd, fw, seq, GRID_W, 2 * tm_lat,
                  final_norm=(i == depth - 1))
        if need_ctx:
            yf_c = _fft(f_c, cseq)
            xc = _merge(y_c, yf_c, uv_c, g_c, xc, mod3, ctx_row, w1, w2, w3, wo, ws, bs, cseq, tm_ctx)
            xc = _ffn(xc, mod3, ctx_row, nw2, wu, fcw, fcb, wd, fw, cseq, cseq, tm_ctx, final_norm=False)
    return xl.reshape(nb, seq, d)
```
